```python
import jax, jax.numpy as jnp
from jax import lax
import numpy as np

D_MODEL = 1024
BATCH = 8
SEQ = 2048
DEPTH = 4

DN_HEADS = 4
DN_HEAD_DIM = 128
DN_WIDTH = DN_HEADS * DN_HEAD_DIM
CONV_WIDTH = 4
CHUNK = 64
SW_Q_HEADS = 8
SW_KV_HEADS = 2
SW_GROUP = SW_Q_HEADS // SW_KV_HEADS
SW_HEAD_DIM = 64
SW_WIDTH = SW_Q_HEADS * SW_HEAD_DIM
SW_KV_WIDTH = SW_KV_HEADS * SW_HEAD_DIM
WINDOW = 128
ROPE_THETA = 10000.0
MIX_WIDTH = DN_WIDTH + SW_WIDTH
D_FF = 3584
N_EXPERTS = 8
TOP_K = 2
N_DENSE = (DEPTH + 1) // 2
N_MOE = DEPTH // 2
ALPHA = (2.0 * DEPTH) ** 0.25
BETA_INIT = (8.0 * DEPTH) ** -0.25
LN_EPS = 1e-5
RMS_EPS = 1e-6
IN_SIZES = (DN_WIDTH, DN_WIDTH, DN_WIDTH, DN_WIDTH, DN_HEADS, DN_HEADS,
            SW_WIDTH, SW_KV_WIDTH, SW_KV_WIDTH)
IN_COLS = sum(IN_SIZES)

kernel_name = "hymba_deltanet_swa_sinks_moe_deepnorm"


def layer_norm(x, g, b):
    xf = x.astype(jnp.float32)
    mu = jnp.mean(xf, axis=-1, keepdims=True)
    var = jnp.mean(jnp.square(xf - mu), axis=-1, keepdims=True)
    y = (xf - mu) * lax.rsqrt(var + LN_EPS) * g.astype(jnp.float32) + b.astype(jnp.float32)
    return y.astype(x.dtype)


def l2norm(x):
    return x * lax.rsqrt(jnp.sum(x * x, axis=-1, keepdims=True) + 1e-6)


def causal_conv_silu(x, w):
    S = x.shape[1]
    xp = jnp.pad(x, ((0, 0), (CONV_WIDTH - 1, 0), (0, 0)))
    y = sum(xp[:, k:k + S] * w[k] for k in range(CONV_WIDTH))
    return jax.nn.silu(y)


def chunk_gated_delta_rule(q, k, v, g, beta):
    B, S, H, dk = q.shape
    dv = v.shape[-1]
    N = S // CHUNK
    f32 = jnp.float32
    q = l2norm(q.astype(f32)) * (dk ** -0.5)
    k = l2norm(k.astype(f32))
    v = v.astype(f32)

    def chunks(t):
        t = t.reshape((B, N, CHUNK, H) + t.shape[3:])
        return jnp.moveaxis(t, 3, 1)

    q, k, v = chunks(q), chunks(k), chunks(v)
    g = chunks(g.astype(f32))
    beta = chunks(beta.astype(f32))
    g_cum = jnp.cumsum(g, axis=-1)
    k_beta = k * beta[..., None]
    v_beta = v * beta[..., None]

    idx = jnp.arange(CHUNK)
    lower_incl = idx[:, None] >= idx[None, :]
    strict_lower = idx[:, None] > idx[None, :]
    decay = jnp.exp(jnp.where(lower_incl, g_cum[..., :, None] - g_cum[..., None, :], -jnp.inf))

    L = jnp.where(strict_lower, jnp.einsum('bhncd,bhnjd->bhncj', k_beta, k) * decay, 0.0)
    eye = jnp.eye(CHUNK, dtype=f32)
    T = lax.linalg.triangular_solve(eye + L, jnp.broadcast_to(eye, L.shape),
                                    left_side=True, lower=True, unit_diagonal=True)
    u = jnp.einsum('bhncj,bhnjd->bhncd', T, v_beta)
    w = jnp.einsum('bhncj,bhnjd->bhncd', T, k_beta * jnp.exp(g_cum)[..., None])
    attn_intra = jnp.where(lower_incl, jnp.einsum('bhncd,bhnjd->bhncj', q, k) * decay, 0.0)
    q_dec = q * jnp.exp(g_cum)[..., None]
    k_end = k * jnp.exp(g_cum[..., -1:] - g_cum)[..., None]
    g_last = jnp.exp(g_cum[..., -1])

    def step(state, inp):
        u_n, w_n, qd_n, ke_n, a_n, gl_n = inp
        v_new = u_n - jnp.einsum('bhck,bhkv->bhcv', w_n, state)
        o = jnp.einsum('bhck,bhkv->bhcv', qd_n, state) + jnp.einsum('bhcj,bhjv->bhcv', a_n, v_new)
        state = state * gl_n[..., None, None] + jnp.einsum('bhck,bhcv->bhkv', ke_n, v_new)
        return state, o

    xs = tuple(jnp.moveaxis(t, 2, 0) for t in (u, w, q_dec, k_end, attn_intra, g_last))
    state0 = jnp.zeros((B, H, dk, dv), f32)
    _, o = lax.scan(step, state0, xs)
    o = jnp.moveaxis(o, 0, 2)
    return jnp.moveaxis(o, 1, 3).reshape(B, S, H, dv)


def rope_tables(positions):
    inv_freq = ROPE_THETA ** (-jnp.arange(0, SW_HEAD_DIM, 2, dtype=jnp.float32) / SW_HEAD_DIM)
    ang = positions.astype(jnp.float32)[..., None] * inv_freq
    return jnp.cos(ang)[:, :, None, :], jnp.sin(ang)[:, :, None, :]


def apply_rope(x, cos, sin):
    xf = x.astype(jnp.float32)
    x1, x2 = jnp.split(xf, 2, axis=-1)
    return jnp.concatenate([x1 * cos - x2 * sin, x2 * cos + x1 * sin], axis=-1).astype(x.dtype)


def sliding_window_attention(q, k, v, sinks, cos, sin):
    B, S = q.shape[:2]
    BLK = WINDOW
    NB = S // BLK
    q = apply_rope(q, cos, sin)
    k = apply_rope(k, cos, sin)
    qb = q.reshape(B, NB, BLK, SW_KV_HEADS, SW_GROUP, SW_HEAD_DIM)

    def band(t):
        tb = t.reshape(B, NB, BLK, SW_KV_HEADS, SW_HEAD_DIM)
        prev = jnp.concatenate([jnp.zeros_like(tb[:, :1]), tb[:, :-1]], axis=1)
        return jnp.concatenate([prev, tb], axis=2)

    kband, vband = band(k), band(v)
    s = jnp.einsum('bnqhgd,bnkhd->bnhgqk', qb, kband).astype(jnp.float32) * (SW_HEAD_DIM ** -0.5)
    qi = jnp.arange(BLK)[:, None]
    kj = jnp.arange(2 * BLK)[None, :]
    rel = qi + BLK - kj
    in_window = (rel >= 0) & (rel < WINDOW)
    exists = (jnp.arange(NB)[:, None, None] > 0) | (kj >= BLK)[None]
    valid = in_window[None] & exists
    s = jnp.where(valid[None, :, None, None], s, -jnp.inf)
    sink = sinks.astype(jnp.float32).reshape(SW_KV_HEADS, SW_GROUP)[None, None, :, :, None, None]
    m = jnp.maximum(jnp.max(s, axis=-1, keepdims=True), sink)
    p = jnp.exp(s - m)
    denom = jnp.sum(p, axis=-1, keepdims=True) + jnp.exp(sink - m)
    probs = (p / denom).astype(v.dtype)
    o = jnp.einsum('bnhgqk,bnkhd->bnqhgd', probs, vband)
    return o.reshape(B, S, SW_WIDTH)


def hybrid_mixer(x, cos, sin, w_in, conv_w, a_log, dt_bias, dn_norm_w, sinks, w_out):
    B, S, _ = x.shape
    proj = x @ w_in
    points = [int(c) for c in np.cumsum(IN_SIZES)[:-1]]
    q_dn, k_dn, v_dn, z_dn, a_dn, b_dn, q_sw, k_sw, v_sw = jnp.split(proj, points, axis=-1)

    qkv = causal_conv_silu(jnp.concatenate([q_dn, k_dn, v_dn], axis=-1), conv_w)
    q_dn, k_dn, v_dn = jnp.split(qkv, 3, axis=-1)
    heads = lambda t: t.reshape(B, S, DN_HEADS, DN_HEAD_DIM)
    g = -jnp.exp(a_log.astype(jnp.float32)) * jax.nn.softplus(
        a_dn.astype(jnp.float32) + dt_bias.astype(jnp.float32))
    beta = jax.nn.sigmoid(b_dn.astype(jnp.float32))
    o_dn = chunk_gated_delta_rule(heads(q_dn), heads(k_dn), heads(v_dn), g, beta)
    o_dn = o_dn * lax.rsqrt(jnp.mean(o_dn * o_dn, axis=-1, keepdims=True) + RMS_EPS)
    o_dn = o_dn * dn_norm_w.astype(jnp.float32) * jax.nn.silu(heads(z_dn).astype(jnp.float32))
    o_dn = o_dn.reshape(B, S, DN_WIDTH).astype(x.dtype)

    o_sw = sliding_window_attention(
        q_sw.reshape(B, S, SW_Q_HEADS, SW_HEAD_DIM),
        k_sw.reshape(B, S, SW_KV_HEADS, SW_HEAD_DIM),
        v_sw.reshape(B, S, SW_KV_HEADS, SW_HEAD_DIM), sinks, cos, sin)

    return jnp.concatenate([o_dn, o_sw], axis=-1) @ w_out


def swiglu(x, w_gate, w_up, w_down):
    return (jax.nn.silu(x @ w_gate) * (x @ w_up)) @ w_down


def moe_swiglu(x, router_w, w_gate, w_up, w_down):
    B, S, D = x.shape
    xt = x.reshape(-1, D)
    logits = (xt @ router_w).astype(jnp.float32)
    top_vals, top_idx = lax.top_k(logits, TOP_K)
    gates = jax.nn.softmax(top_vals, axis=-1)
    combine = jnp.sum(jax.nn.one_hot(top_idx, N_EXPERTS, dtype=jnp.float32) * gates[..., None], axis=1)
    combine = combine.astype(x.dtype)
    out = jnp.zeros_like(xt)
    for e in range(N_EXPERTS):
        out = out + combine[:, e:e + 1] * swiglu(xt, w_gate[e], w_up[e], w_down[e])
    return out.reshape(B, S, D)


def setup_inputs(seed: int = 0) -> dict:
    key = jax.random.key(seed)
    ks = jax.random.split(key, 20)
    f32 = jnp.float32
    nrm = lambda k, shape, scale: jax.random.normal(k, shape, f32) * scale
    x = jax.random.normal(ks[0], (BATCH, SEQ, D_MODEL), f32)
    positions = jnp.broadcast_to(jnp.arange(SEQ, dtype=jnp.int32)[None, :], (BATCH, SEQ))
    col_scale = jnp.concatenate([
        jnp.ones((2 * DN_WIDTH,), f32), jnp.full((DN_WIDTH,), BETA_INIT, f32),
        jnp.ones((DN_WIDTH + 2 * DN_HEADS + SW_WIDTH + SW_KV_WIDTH,), f32),
        jnp.full((SW_KV_WIDTH,), BETA_INIT, f32)])
    w_in = nrm(ks[1], (DEPTH, D_MODEL, IN_COLS), D_MODEL ** -0.5) * col_scale
    conv_w = nrm(ks[2], (DEPTH, CONV_WIDTH, 3 * DN_WIDTH), CONV_WIDTH ** -0.5)
    a_log = jnp.log(jax.random.uniform(ks[3], (DEPTH, DN_HEADS), f32, minval=1.0, maxval=16.0))
    dt = jnp.exp(jax.random.uniform(ks[4], (DEPTH, DN_HEADS), f32,
                                    minval=float(np.log(1e-3)), maxval=float(np.log(1e-1))))
    dt_bias = dt + jnp.log(-jnp.expm1(-dt))
    dn_norm_w = 1.0 + nrm(ks[5], (DEPTH, DN_HEAD_DIM), 0.02)
    sinks = nrm(ks[6], (DEPTH, SW_Q_HEADS), 0.5)
    w_out = nrm(ks[7], (DEPTH, MIX_WIDTH, D_MODEL), MIX_WIDTH ** -0.5 * BETA_INIT)
    ln_g = 1.0 + nrm(ks[8], (DEPTH, 2, D_MODEL), 0.02)
    ln_b = nrm(ks[9], (DEPTH, 2, D_MODEL), 0.02)
    ffn_w_gate = nrm(ks[10], (N_DENSE, D_MODEL, D_FF), D_MODEL ** -0.5 * BETA_INIT)
    ffn_w_up = nrm(ks[11], (N_DENSE, D_MODEL, D_FF), D_MODEL ** -0.5 * BETA_INIT)
    ffn_w_down = nrm(ks[12], (N_DENSE, D_FF, D_MODEL), D_FF ** -0.5 * BETA_INIT)
    router_w = nrm(ks[13], (N_MOE, D_MODEL, N_EXPERTS), D_MODEL ** -0.5)
    moe_w_gate = nrm(ks[14], (N_MOE, N_EXPERTS, D_MODEL, D_FF), D_MODEL ** -0.5 * BETA_INIT)
    moe_w_up = nrm(ks[15], (N_MOE, N_EXPERTS, D_MODEL, D_FF), D_MODEL ** -0.5 * BETA_INIT)
    moe_w_down = nrm(ks[16], (N_MOE, N_EXPERTS, D_FF, D_MODEL), D_FF ** -0.5 * BETA_INIT)
    return {"x": x, "positions": positions, "w_in": w_in, "conv_w": conv_w, "a_log": a_log,
            "dt_bias": dt_bias, "dn_norm_w": dn_norm_w, "sinks": sinks, "w_out": w_out,
            "ln_g": ln_g, "ln_b": ln_b, "ffn_w_gate": ffn_w_gate, "ffn_w_up": ffn_w_up,
            "ffn_w_down": ffn_w_down, "router_w": router_w, "moe_w_gate": moe_w_gate,
            "moe_w_up": moe_w_up, "moe_w_down": moe_w_down}


def reference(x, positions, w_in, conv_w, a_log, dt_bias, dn_norm_w, sinks, w_out,
              ln_g, ln_b, ffn_w_gate, ffn_w_up, ffn_w_down, router_w, moe_w_gate,
              moe_w_up, moe_w_down):
    cos, sin = rope_tables(positions)
    for layer in range(DEPTH):
        mix = hybrid_mixer(x, cos, sin, w_in[layer], conv_w[layer], a_log[layer],
                           dt_bias[layer], dn_norm_w[layer], sinks[layer], w_out[layer])
        x = layer_norm(ALPHA * x + mix, ln_g[layer, 0], ln_b[layer, 0])
        if layer % 2 == 0:
            i = layer // 2
            f = swiglu(x, ffn_w_gate[i], ffn_w_up[i], ffn_w_down[i])
        else:
            i = layer // 2
            f = moe_swiglu(x, router_w[i], moe_w_gate[i], moe_w_up[i], moe_w_down[i])
        x = layer_norm(ALPHA * x + f, ln_g[layer, 1], ln_b[layer, 1])
    return x
```

```python
import functools

import jax
import jax.numpy as jnp
import numpy as np
from jax import lax
from jax.experimental import pallas as pl
from jax.experimental.pallas import tpu as pltpu

F32 = jnp.float32
BF16 = jnp.bfloat16

D_MODEL = 1024
DEPTH = 4
DN_HEADS = 4
DN_HEAD_DIM = 128
DN_WIDTH = DN_HEADS * DN_HEAD_DIM
CONV_WIDTH = 4
CHUNK = 64
SW_Q_HEADS = 8
SW_KV_HEADS = 2
SW_GROUP = SW_Q_HEADS // SW_KV_HEADS
SW_HEAD_DIM = 64
SW_WIDTH = SW_Q_HEADS * SW_HEAD_DIM
SW_KV_WIDTH = SW_KV_HEADS * SW_HEAD_DIM
WINDOW = 128
ROPE_THETA = 10000.0
D_FF = 3584
N_EXPERTS = 8
ALPHA = (2.0 * DEPTH) ** 0.25
LN_EPS = 1e-5
RMS_EPS = 1e-6

LANES = 128
SUBLANES = 8
CONV_HALO = SUBLANES
VMEM_LIMIT = 56 * 1024 * 1024

C_QKV = (0, 3 * DN_WIDTH)
C_Z = (C_QKV[1], C_QKV[1] + DN_WIDTH)
C_QSW = (C_Z[1], C_Z[1] + SW_WIDTH)
C_KVSW = (C_QSW[1], C_QSW[1] + 2 * SW_KV_WIDTH)
C_AB = (C_KVSW[1], C_KVSW[1] + LANES)
IN_COLS_PACKED = C_AB[1]

TM_PROJ = 512
TM_FFN = 1024
TF_FFN = 512
TM_ROUTE = 1024
TM_EXPERT = 1024
TM_MOVE = 256


def _cparams(sem):
    return pltpu.CompilerParams(dimension_semantics=sem, vmem_limit_bytes=VMEM_LIMIT)


def _bdot(a, b):
    return jnp.dot(a.astype(BF16), b.astype(BF16), preferred_element_type=F32)


def _bdot_nt(a, b):
    return lax.dot_general(a.astype(BF16), b.astype(BF16), (((1,), (1,)), ((), ())),
                           preferred_element_type=F32)


def _bdot_tn(a, b):
    return lax.dot_general(a.astype(BF16), b.astype(BF16), (((0,), (0,)), ((), ())),
                           preferred_element_type=F32)


def _silu(x):
    return x * jax.nn.sigmoid(x)


def _softplus(x):
    return jnp.maximum(x, 0.0) + jnp.log1p(jnp.exp(-jnp.abs(x)))


def _layer_norm(y, g, b):
    mu = jnp.mean(y, axis=-1, keepdims=True)
    d = y - mu
    var = jnp.mean(d * d, axis=-1, keepdims=True)
    return d * lax.rsqrt(var + LN_EPS) * g + b


def _rope_table_kernel(pos_ref, invf_ref, sign_ref, cos_ref, sin_ref):
    ang = pos_ref[...].astype(F32) * invf_ref[...]
    cos_ref[...] = jnp.cos(ang)
    sin_ref[...] = jnp.sin(ang) * sign_ref[...]


def _rope_tables(positions):
    t = positions.size
    half = SW_HEAD_DIM // 2
    inv_freq = ROPE_THETA ** (-jnp.arange(0, SW_HEAD_DIM, 2, dtype=F32) / SW_HEAD_DIM)
    reps = LANES // half
    invf = jnp.tile(inv_freq, reps).reshape(1, LANES)
    sign = jnp.tile(jnp.concatenate([-jnp.ones((half,), F32), jnp.ones((half,), F32)]),
                    LANES // SW_HEAD_DIM).reshape(1, LANES)
    tm = min(TM_PROJ, t)
    row = pl.BlockSpec((1, LANES), lambda i: (0, 0))
    tab = pl.BlockSpec((tm, LANES), lambda i: (i, 0))
    return pl.pallas_call(
        _rope_table_kernel,
        grid=(t // tm,),
        in_specs=[pl.BlockSpec((tm, 1), lambda i: (i, 0)), row, row],
        out_specs=[tab, tab],
        out_shape=[jax.ShapeDtypeStruct((t, LANES), F32)] * 2,
        compiler_params=_cparams(("parallel",)),
        name="rope_tables",
    )(positions.reshape(t, 1), invf, sign)


def _rope(x, cos, sin_signed):
    width = x.shape[-1]
    half = SW_HEAD_DIM // 2
    lane = lax.broadcasted_iota(jnp.int32, x.shape, 1)
    first = (lane % SW_HEAD_DIM) < half
    partner = jnp.where(first, pltpu.roll(x, width - half, 1), pltpu.roll(x, half, 1))
    return x * cos + partner * sin_signed


def _inproj_kernel(x_ref, w_ref, wabt_ref, cos_ref, sin_ref,
                   qkv_ref, z_ref, qsw_ref, kvsw_ref, ab_ref, abt_ref):
    xb = x_ref[...].astype(BF16)

    def mm(cols):
        return jnp.dot(xb, w_ref[:, cols[0]:cols[1]], preferred_element_type=F32)

    qkv_ref[...] = mm(C_QKV).astype(BF16)
    z_ref[...] = mm(C_Z).astype(BF16)
    cos = cos_ref[...]
    sin = sin_ref[...]
    reps = SW_WIDTH // LANES
    qsw_ref[...] = _rope(mm(C_QSW), jnp.tile(cos, (1, reps)), jnp.tile(sin, (1, reps))).astype(BF16)
    kv = mm(C_KVSW)
    k = _rope(kv[:, :SW_KV_WIDTH], cos, sin)
    kvsw_ref[...] = jnp.concatenate([k, kv[:, SW_KV_WIDTH:]], axis=1).astype(BF16)
    ab_ref[...] = mm(C_AB)
    abt_ref[...] = lax.dot_general(wabt_ref[...], xb, (((1,), (1,)), ((), ())),
                                   preferred_element_type=F32)


def _inproj(x2d, w_packed, wabt, cos, sin):
    t = x2d.shape[0]
    tm = min(TM_PROJ, t)

    def rows(width):
        return pl.BlockSpec((tm, width), lambda i: (i, 0))

    return pl.pallas_call(
        _inproj_kernel,
        grid=(t // tm,),
        in_specs=[rows(D_MODEL),
                  pl.BlockSpec((D_MODEL, IN_COLS_PACKED), lambda i: (0, 0)),
                  pl.BlockSpec((SUBLANES, D_MODEL), lambda i: (0, 0)),
                  rows(LANES), rows(LANES)],
        out_specs=[rows(3 * DN_WIDTH), rows(DN_WIDTH), rows(SW_WIDTH), rows(2 * SW_KV_WIDTH),
                   rows(LANES), pl.BlockSpec((SUBLANES, tm), lambda i: (0, i))],
        out_shape=[jax.ShapeDtypeStruct((t, 3 * DN_WIDTH), BF16),
                   jax.ShapeDtypeStruct((t, DN_WIDTH), BF16),
                   jax.ShapeDtypeStruct((t, SW_WIDTH), BF16),
                   jax.ShapeDtypeStruct((t, 2 * SW_KV_WIDTH), BF16),
                   jax.ShapeDtypeStruct((t, LANES), F32),
                   jax.ShapeDtypeStruct((SUBLANES, t), F32)],
        compiler_params=_cparams(("parallel",)),
        name="in_proj",
    )(x2d, w_packed, wabt, cos, sin)


def _unit_lower_inverse(lmat, row, col):
    same16 = (row // 16) == (col // 16)
    same32 = (row // 32) == (col // 32)
    eye = (row == col).astype(F32)
    ld = jnp.where(same16, lmat, 0.0)
    inv = eye - ld
    power = ld
    for _ in range(3):
        power = _bdot(power, power)
        inv = inv + _bdot(inv, power)
    for off_diag in (same32 & ~same16, ~same32):
        c = jnp.where(off_diag, lmat, 0.0)
        inv = inv - _bdot(_bdot(inv, c), inv)
    return inv


def _deltanet_kernel(qkv_ref, z_ref, ab_ref, abt_ref, convw_ref, prow_ref, pcol_ref, normw_ref,
                     o_ref, xp_ref, state_ref):
    @pl.when(pl.program_id(1) == 0)
    def _():
        xp_ref[0:CONV_HALO, :] = jnp.zeros((CONV_HALO, 3 * DN_WIDTH), F32)
        state_ref[...] = jnp.zeros_like(state_ref)

    xp_ref[CONV_HALO:CONV_HALO + CHUNK, :] = qkv_ref[...].astype(F32)
    convw = convw_ref[...]
    y = None
    for tap in range(CONV_WIDTH):
        start = CONV_HALO - (CONV_WIDTH - 1) + tap
        term = xp_ref[start:start + CHUNK, :] * convw[tap:tap + 1, :]
        y = term if y is None else y + term
    xp_ref[0:CONV_HALO, :] = xp_ref[CHUNK:CHUNK + CONV_HALO, :]
    qkv = _silu(y)

    row = lax.broadcasted_iota(jnp.int32, (CHUNK, CHUNK), 0)
    col = lax.broadcasted_iota(jnp.int32, (CHUNK, CHUNK), 1)
    lower_incl = row >= col
    strict_lower = row > col
    tri = lower_incl.astype(F32)
    tri_t = (row <= col).astype(F32)

    ab = ab_ref[...]
    g_cols = -jnp.exp(prow_ref[0:1, :]) * _softplus(ab + prow_ref[1:2, :])
    abt = abt_ref[0]
    g_rows = -jnp.exp(pcol_ref[:, 0:1]) * _softplus(abt + pcol_ref[:, 1:2])
    gcum_cols = jnp.dot(tri, g_cols, preferred_element_type=F32, precision=lax.Precision.HIGHEST)
    gcum_rows = jnp.dot(g_rows, tri_t, preferred_element_type=F32, precision=lax.Precision.HIGHEST)
    beta_cols = jax.nn.sigmoid(ab)

    normw = normw_ref[...]
    outs = []
    for h in range(DN_HEADS):
        sl = slice(h * DN_HEAD_DIM, (h + 1) * DN_HEAD_DIM)
        q = qkv[:, sl]
        k = qkv[:, DN_WIDTH + h * DN_HEAD_DIM:DN_WIDTH + (h + 1) * DN_HEAD_DIM]
        v = qkv[:, 2 * DN_WIDTH + h * DN_HEAD_DIM:2 * DN_WIDTH + (h + 1) * DN_HEAD_DIM]
        q = q * lax.rsqrt(jnp.sum(q * q, axis=-1, keepdims=True) + 1e-6) * (DN_HEAD_DIM ** -0.5)
        k = k * lax.rsqrt(jnp.sum(k * k, axis=-1, keepdims=True) + 1e-6)
        gc = gcum_cols[:, h:h + 1]
        gr = gcum_rows[h:h + 1, :]
        g_last = gcum_cols[CHUNK - 1:CHUNK, h:h + 1]
        beta = beta_cols[:, DN_HEADS + h:DN_HEADS + h + 1]
        decay = jnp.where(lower_incl, jnp.exp(jnp.where(lower_incl, gc - gr, 0.0)), 0.0)
        k_beta = k * beta
        lmat = jnp.where(strict_lower, _bdot_nt(k_beta, k) * decay, 0.0)
        tinv = _unit_lower_inverse(lmat, row, col)
        exp_gc = jnp.exp(gc)
        u = _bdot(tinv, v * beta)
        w = _bdot(tinv, k_beta * exp_gc)
        attn = jnp.where(lower_incl, _bdot_nt(q, k) * decay, 0.0)
        q_dec = q * exp_gc
        k_end = k * jnp.exp(g_last - gc)
        state = state_ref[h]
        v_new = u - _bdot(w, state)
        o = _bdot(q_dec, state) + _bdot(attn, v_new)
        state_ref[h] = state * jnp.exp(g_last) + _bdot_tn(k_end, v_new)
        o = o * lax.rsqrt(jnp.mean(o * o, axis=-1, keepdims=True) + RMS_EPS)
        outs.append(o * normw * _silu(z_ref[:, sl].astype(F32)))
    o_ref[...] = jnp.concatenate(outs, axis=1).astype(o_ref.dtype)


def _deltanet(qkv, z, ab, abt, conv_w, a_log, dt_bias, dn_norm_w, batch, seq):
    t = batch * seq
    nchunk = seq // CHUNK
    abt3 = abt.reshape(SUBLANES, t // CHUNK, CHUNK).transpose(1, 0, 2)
    prow = jnp.zeros((SUBLANES, LANES), F32).at[0, :DN_HEADS].set(a_log).at[1, :DN_HEADS].set(dt_bias)
    pcol = (jnp.zeros((SUBLANES, LANES), F32)
            .at[:DN_HEADS, 0].set(a_log).at[DN_HEADS:2 * DN_HEADS, 0].set(a_log)
            .at[:DN_HEADS, 1].set(dt_bias).at[DN_HEADS:2 * DN_HEADS, 1].set(dt_bias))

    def rows(width):
        return pl.BlockSpec((CHUNK, width), lambda b, c: (b * nchunk + c, 0))

    def const(shape):
        return pl.BlockSpec(shape, lambda b, c: (0,) * len(shape))

    return pl.pallas_call(
        _deltanet_kernel,
        grid=(batch, nchunk),
        in_specs=[rows(3 * DN_WIDTH), rows(DN_WIDTH), rows(LANES),
                  pl.BlockSpec((1, SUBLANES, CHUNK), lambda b, c: (b * nchunk + c, 0, 0)),
                  const((CONV_WIDTH, 3 * DN_WIDTH)), const((SUBLANES, LANES)),
                  const((SUBLANES, LANES)), const((1, DN_HEAD_DIM))],
        out_specs=rows(DN_WIDTH),
        out_shape=jax.ShapeDtypeStruct((t, DN_WIDTH), BF16),
        scratch_shapes=[pltpu.VMEM((CONV_HALO + CHUNK, 3 * DN_WIDTH), F32),
                        pltpu.VMEM((DN_HEADS, DN_HEAD_DIM, DN_HEAD_DIM), F32)],
        compiler_params=_cparams(("parallel", "arbitrary")),
        name="gated_deltanet",
    )(qkv, z, ab, abt3, conv_w, prow, pcol, dn_norm_w.reshape(1, DN_HEAD_DIM))


def _swa_kernel(q_ref, kvc_ref, kvp_ref, sink_ref, o_ref):
    blk = pl.program_id(1)
    qi = lax.broadcasted_iota(jnp.int32, (WINDOW, 2 * WINDOW), 0)
    kj = lax.broadcasted_iota(jnp.int32, (WINDOW, 2 * WINDOW), 1)
    rel = qi + WINDOW - kj
    valid = (rel >= 0) & (rel < WINDOW) & ((kj >= WINDOW) | (blk > 0))
    kv_band = jnp.concatenate([kvp_ref[...], kvc_ref[...]], axis=0)
    scale = SW_HEAD_DIM ** -0.5
    outs = []
    for hq in range(SW_Q_HEADS):
        hk = hq // SW_GROUP
        q = q_ref[:, hq * SW_HEAD_DIM:(hq + 1) * SW_HEAD_DIM]
        k = kv_band[:, hk * SW_HEAD_DIM:(hk + 1) * SW_HEAD_DIM]
        v = kv_band[:, SW_KV_WIDTH + hk * SW_HEAD_DIM:SW_KV_WIDTH + (hk + 1) * SW_HEAD_DIM]
        s = _bdot_nt(q, k) * scale
        s = jnp.where(valid, s, -jnp.inf)
        sink = sink_ref[0:1, hq:hq + 1]
        m = jnp.maximum(jnp.max(s, axis=-1, keepdims=True), sink)
        p = jnp.exp(s - m)
        denom = jnp.sum(p, axis=-1, keepdims=True) + jnp.exp(sink - m)
        outs.append(_bdot(p / denom, v))
    o_ref[...] = jnp.concatenate(outs, axis=1).astype(o_ref.dtype)


def _swa(qsw, kvsw, sinks, batch, seq):
    t = batch * seq
    nblk = seq // WINDOW
    return pl.pallas_call(
        _swa_kernel,
        grid=(batch, nblk),
        in_specs=[pl.BlockSpec((WINDOW, SW_WIDTH), lambda b, i: (b * nblk + i, 0)),
                  pl.BlockSpec((WINDOW, 2 * SW_KV_WIDTH), lambda b, i: (b * nblk + i, 0)),
                  pl.BlockSpec((WINDOW, 2 * SW_KV_WIDTH),
                               lambda b, i: (b * nblk + jnp.maximum(i - 1, 0), 0)),
                  pl.BlockSpec((1, SW_Q_HEADS), lambda b, i: (0, 0))],
        out_specs=pl.BlockSpec((WINDOW, SW_WIDTH), lambda b, i: (b * nblk + i, 0)),
        out_shape=jax.ShapeDtypeStruct((t, SW_WIDTH), BF16),
        compiler_params=_cparams(("parallel", "parallel")),
        name="sliding_window_attn",
    )(qsw, kvsw, kvsw, sinks.reshape(1, SW_Q_HEADS))


def _outproj_kernel(odn_ref, osw_ref, w_ref, x_ref, g_ref, b_ref, o_ref):
    mix = (jnp.dot(odn_ref[...], w_ref[0:DN_WIDTH, :], preferred_element_type=F32)
           + jnp.dot(osw_ref[...], w_ref[DN_WIDTH:, :], preferred_element_type=F32))
    o_ref[...] = _layer_norm(ALPHA * x_ref[...] + mix, g_ref[...], b_ref[...])


def _outproj_ln(o_dn, o_sw, w_out_bf16, x2d, g, b):
    t = x2d.shape[0]
    tm = min(TM_PROJ, t)

    def rows(width):
        return pl.BlockSpec((tm, width), lambda i: (i, 0))

    vec = pl.BlockSpec((1, D_MODEL), lambda i: (0, 0))
    return pl.pallas_call(
        _outproj_kernel,
        grid=(t // tm,),
        in_specs=[rows(DN_WIDTH), rows(SW_WIDTH),
                  pl.BlockSpec((DN_WIDTH + SW_WIDTH, D_MODEL), lambda i: (0, 0)),
                  rows(D_MODEL), vec, vec],
        out_specs=rows(D_MODEL),
        out_shape=jax.ShapeDtypeStruct((t, D_MODEL), F32),
        compiler_params=_cparams(("parallel",)),
        name="out_proj_ln",
    )(o_dn, o_sw, w_out_bf16, x2d, g.reshape(1, D_MODEL), b.reshape(1, D_MODEL))


def _ffn_kernel(x_ref, wg_ref, wu_ref, wd_ref, g_ref, b_ref, o_ref, xb_ref, acc_ref):
    f = pl.program_id(1)

    @pl.when(f == 0)
    def _():
        xb_ref[...] = x_ref[...].astype(BF16)
        acc_ref[...] = jnp.zeros_like(acc_ref)

    xb = xb_ref[...]
    gate = jnp.dot(xb, wg_ref[...].astype(BF16), preferred_element_type=F32)
    up = jnp.dot(xb, wu_ref[...].astype(BF16), preferred_element_type=F32)
    hid = (_silu(gate) * up).astype(BF16)
    acc_ref[...] += jnp.dot(hid, wd_ref[...].astype(BF16), preferred_element_type=F32)

    @pl.when(f == pl.num_programs(1) - 1)
    def _():
        o_ref[...] = _layer_norm(ALPHA * x_ref[...] + acc_ref[...], g_ref[...], b_ref[...])


def _ffn_ln(x2d, w_gate, w_up, w_down, g, b):
    t = x2d.shape[0]
    tm = min(TM_FFN, t)
    vec = pl.BlockSpec((1, D_MODEL), lambda i, f: (0, 0))
    return pl.pallas_call(
        _ffn_kernel,
        grid=(t // tm, D_FF // TF_FFN),
        in_specs=[pl.BlockSpec((tm, D_MODEL), lambda i, f: (i, 0)),
                  pl.BlockSpec((D_MODEL, TF_FFN), lambda i, f: (0, f)),
                  pl.BlockSpec((D_MODEL, TF_FFN), lambda i, f: (0, f)),
                  pl.BlockSpec((TF_FFN, D_MODEL), lambda i, f: (f, 0)),
                  vec, vec],
        out_specs=pl.BlockSpec((tm, D_MODEL), lambda i, f: (i, 0)),
        out_shape=jax.ShapeDtypeStruct((t, D_MODEL), F32),
        scratch_shapes=[pltpu.VMEM((tm, D_MODEL), BF16), pltpu.VMEM((tm, D_MODEL), F32)],
        compiler_params=_cparams(("parallel", "arbitrary")),
        name="dense_swiglu_ln",
    )(x2d, w_gate, w_up, w_down, g.reshape(1, D_MODEL), b.reshape(1, D_MODEL))


R_E1, R_E2, R_RANK1, R_RANK2, R_G1, R_G2 = range(6)


def _router_kernel(x_ref, wr_ref, meta_ref, count_ref, carry_ref):
    i = pl.program_id(0)

    @pl.when(i == 0)
    def _():
        carry_ref[...] = jnp.zeros_like(carry_ref)

    tm = x_ref.shape[0]
    logits = jnp.dot(x_ref[...], wr_ref[...], preferred_element_type=F32,
                     precision=lax.Precision.HIGHEST)
    lane = lax.broadcasted_iota(jnp.int32, logits.shape, 1)
    logits = jnp.where(lane < N_EXPERTS, logits, -jnp.inf)
    m1 = jnp.max(logits, axis=-1, keepdims=True)
    e1 = jnp.min(jnp.where(logits == m1, lane, LANES), axis=-1, keepdims=True)
    rest = jnp.where(lane == e1, -jnp.inf, logits)
    m2 = jnp.max(rest, axis=-1, keepdims=True)
    e2 = jnp.min(jnp.where(rest == m2, lane, LANES), axis=-1, keepdims=True)
    ex = jnp.exp(m2 - m1)
    g1 = 1.0 / (1.0 + ex)
    g2 = ex / (1.0 + ex)

    hit = ((lane == e1) | (lane == e2)).astype(BF16)
    r = lax.broadcasted_iota(jnp.int32, (tm, tm), 0)
    c = lax.broadcasted_iota(jnp.int32, (tm, tm), 1)
    before = jnp.dot((c < r).astype(BF16), hit, preferred_element_type=F32) + carry_ref[0:1, :]
    rank1 = jnp.sum(jnp.where(lane == e1, before, 0.0), axis=-1, keepdims=True)
    rank2 = jnp.sum(jnp.where(lane == e2, before, 0.0), axis=-1, keepdims=True)
    carry_ref[0:1, :] = carry_ref[0:1, :] + jnp.sum(hit.astype(F32), axis=0, keepdims=True)
    count_ref[...] = jnp.broadcast_to(carry_ref[0:1, :], count_ref.shape)

    meta = jnp.zeros((tm, SUBLANES), F32)
    lane8 = lax.broadcasted_iota(jnp.int32, (tm, SUBLANES), 1)
    for idx, val in ((R_E1, e1.astype(F32)), (R_E2, e2.astype(F32)), (R_RANK1, rank1),
                     (R_RANK2, rank2), (R_G1, g1), (R_G2, g2)):
        meta = jnp.where(lane8 == idx, val, meta)
    meta_ref[...] = meta


def _router(x2d, router_w):
    t = x2d.shape[0]
    tm = min(TM_ROUTE, t)
    wr = jnp.zeros((D_MODEL, LANES), F32).at[:, :N_EXPERTS].set(router_w)
    return pl.pallas_call(
        _router_kernel,
        grid=(t // tm,),
        in_specs=[pl.BlockSpec((tm, D_MODEL), lambda i: (i, 0)),
                  pl.BlockSpec((D_MODEL, LANES), lambda i: (0, 0))],
        out_specs=[pl.BlockSpec((tm, SUBLANES), lambda i: (i, 0)),
                   pl.BlockSpec((SUBLANES, LANES), lambda i: (0, 0))],
        out_shape=[jax.ShapeDtypeStruct((t, SUBLANES), F32),
                   jax.ShapeDtypeStruct((SUBLANES, LANES), F32)],
        scratch_shapes=[pltpu.VMEM((SUBLANES, LANES), F32)],
        compiler_params=_cparams(("arbitrary",)),
        name="moe_router",
    )(x2d, wr)


def _row_copy(src_ref, src_row, dst_ref, dst_row, sem):
    return pltpu.make_async_copy(src_ref.at[pl.ds(src_row, 1), :], dst_ref.at[pl.ds(dst_row, 1), :], sem)


def _dispatch_kernel(dest1_ref, dest2_ref, x_ref, xs_ref, sem):
    base = pl.program_id(0) * TM_MOVE

    def issue(j, carry):
        tok = base + j
        _row_copy(x_ref, tok, xs_ref, dest1_ref[tok], sem).start()
        _row_copy(x_ref, tok, xs_ref, dest2_ref[tok], sem).start()
        return carry

    lax.fori_loop(0, TM_MOVE, issue, 0)

    def drain(j, carry):
        _row_copy(x_ref, base, xs_ref, 0, sem).wait()
        _row_copy(x_ref, base, xs_ref, 0, sem).wait()
        return carry

    lax.fori_loop(0, TM_MOVE, drain, 0)


def _dispatch(x2d, dest1, dest2, rows_sorted):
    t = x2d.shape[0]
    return pl.pallas_call(
        _dispatch_kernel,
        grid_spec=pltpu.PrefetchScalarGridSpec(
            num_scalar_prefetch=2,
            grid=(t // TM_MOVE,),
            in_specs=[pl.BlockSpec(memory_space=pl.ANY)],
            out_specs=pl.BlockSpec(memory_space=pl.ANY),
            scratch_shapes=[pltpu.SemaphoreType.DMA(())]),
        out_shape=jax.ShapeDtypeStruct((rows_sorted, D_MODEL), F32),
        compiler_params=_cparams(("arbitrary",)),
        name="moe_dispatch",
    )(dest1, dest2, x2d)


def _expert_kernel(vtile_ref, vexpert_ref, vfidx_ref, vlo_ref, vhi_ref, vfirst_ref, vlast_ref,
                   xs_ref, wg_ref, wu_ref, wd_ref, ys_ref, xb_ref, acc_ref):
    v = pl.program_id(0)
    f = pl.program_id(1)
    lo = vlo_ref[v]
    hi = vhi_ref[v]

    @pl.when(hi > lo)
    def _():
        @pl.when(f == 0)
        def _():
            rowid = lax.broadcasted_iota(jnp.int32, xs_ref.shape, 0)
            xb_ref[...] = jnp.where((rowid >= lo) & (rowid < hi), xs_ref[...], 0.0).astype(BF16)

        @pl.when((f == 0) & (vfirst_ref[v] == 1))
        def _():
            acc_ref[...] = jnp.zeros_like(acc_ref)

        xb = xb_ref[...]
        gate = jnp.dot(xb, wg_ref[0].astype(BF16), preferred_element_type=F32)
        up = jnp.dot(xb, wu_ref[0].astype(BF16), preferred_element_type=F32)
        hid = (_silu(gate) * up).astype(BF16)
        acc_ref[...] += jnp.dot(hid, wd_ref[0].astype(BF16), preferred_element_type=F32)

        @pl.when((f == pl.num_programs(1) - 1) & (vlast_ref[v] == 1))
        def _():
            ys_ref[...] = acc_ref[...]


def _expert_visits(counts, n_tiles):
    n_visits = n_tiles + N_EXPERTS - 1
    ends = jnp.cumsum(counts)
    offs = ends - counts
    first_tile = offs // TM_EXPERT
    per_expert = jnp.where(counts > 0, (ends - 1) // TM_EXPERT - first_tile + 1, 0)
    vend = jnp.cumsum(per_expert)
    vstart = vend - per_expert
    total = vend[-1]
    vid = jnp.arange(n_visits, dtype=jnp.int32)
    real = vid < total
    vid_c = jnp.minimum(vid, total - 1)
    vexpert = jnp.minimum(jnp.sum(vid_c[:, None] >= vend[None, :], axis=1), N_EXPERTS - 1)
    vtile = first_tile[vexpert] + vid_c - vstart[vexpert]
    lo = jnp.maximum(offs[vexpert], vtile * TM_EXPERT) - vtile * TM_EXPERT
    hi = jnp.minimum(ends[vexpert], (vtile + 1) * TM_EXPERT) - vtile * TM_EXPERT
    prev_tile = jnp.concatenate([jnp.full((1,), -1, vtile.dtype), vtile[:-1]])
    next_tile = jnp.concatenate([vtile[1:], jnp.full((1,), -1, vtile.dtype)])
    first = real & (vtile != prev_tile)
    last = real & ((vtile != next_tile) | (vid == total - 1))
    i32 = lambda a: a.astype(jnp.int32)
    fpin = jnp.where(real, -1, D_FF // TF_FFN - 1)
    return (i32(vtile), i32(vexpert), i32(fpin), i32(jnp.where(real, lo, 0)),
            i32(jnp.where(real, hi, 0)), i32(first), i32(last), i32(offs))


def _expert_ffn(xs, visits, w_gate, w_up, w_down):
    rows_sorted = xs.shape[0]
    n_visits = visits[0].shape[0]
    nf = D_FF // TF_FFN

    def f_idx(v, f, vf):
        return jnp.where(vf[v] >= 0, vf[v], f)

    def rows_map(v, f, vt, ve, vf, *_):
        return (vt[v], 0)

    def wcol_map(v, f, vt, ve, vf, *_):
        return (ve[v], 0, f_idx(v, f, vf))

    def wrow_map(v, f, vt, ve, vf, *_):
        return (ve[v], f_idx(v, f, vf), 0)

    return pl.pallas_call(
        _expert_kernel,
        grid_spec=pltpu.PrefetchScalarGridSpec(
            num_scalar_prefetch=len(visits),
            grid=(n_visits, nf),
            in_specs=[pl.BlockSpec((TM_EXPERT, D_MODEL), rows_map),
                      pl.BlockSpec((1, D_MODEL, TF_FFN), wcol_map),
                      pl.BlockSpec((1, D_MODEL, TF_FFN), wcol_map),
                      pl.BlockSpec((1, TF_FFN, D_MODEL), wrow_map)],
            out_specs=pl.BlockSpec((TM_EXPERT, D_MODEL), rows_map),
            scratch_shapes=[pltpu.VMEM((TM_EXPERT, D_MODEL), BF16),
                            pltpu.VMEM((TM_EXPERT, D_MODEL), F32)]),
        out_shape=jax.ShapeDtypeStruct((rows_sorted, D_MODEL), F32),
        compiler_params=_cparams(("arbitrary", "arbitrary")),
        name="moe_expert_swiglu",
    )(*visits, xs, w_gate, w_up, w_down)


def _combine_kernel(dest1_ref, dest2_ref, ys_ref, x_ref, meta_ref, g_ref, b_ref, o_ref,
                    buf1_ref, buf2_ref, sem):
    base = pl.program_id(0) * TM_MOVE

    def issue(j, carry):
        tok = base + j
        _row_copy(ys_ref, dest1_ref[tok], buf1_ref, j, sem).start()
        _row_copy(ys_ref, dest2_ref[tok], buf2_ref, j, sem).start()
        return carry

    lax.fori_loop(0, TM_MOVE, issue, 0)

    def drain(j, carry):
        _row_copy(ys_ref, 0, buf1_ref, 0, sem).wait()
        _row_copy(ys_ref, 0, buf2_ref, 0, sem).wait()
        return carry

    lax.fori_loop(0, TM_MOVE, drain, 0)

    meta = meta_ref[...]
    g1 = meta[:, R_G1:R_G1 + 1]
    g2 = meta[:, R_G2:R_G2 + 1]
    moe = g1 * buf1_ref[...] + g2 * buf2_ref[...]
    o_ref[...] = _layer_norm(ALPHA * x_ref[...] + moe, g_ref[...], b_ref[...])


def _combine_ln(ys, dest1, dest2, x2d, meta, g, b):
    t = x2d.shape[0]
    vec = pl.BlockSpec((1, D_MODEL), lambda i, d1, d2: (0, 0))
    return pl.pallas_call(
        _combine_kernel,
        grid_spec=pltpu.PrefetchScalarGridSpec(
            num_scalar_prefetch=2,
            grid=(t // TM_MOVE,),
            in_specs=[pl.BlockSpec(memory_space=pl.ANY),
                      pl.BlockSpec((TM_MOVE, D_MODEL), lambda i, d1, d2: (i, 0)),
                      pl.BlockSpec((TM_MOVE, SUBLANES), lambda i, d1, d2: (i, 0)),
                      vec, vec],
            out_specs=pl.BlockSpec((TM_MOVE, D_MODEL), lambda i, d1, d2: (i, 0)),
            scratch_shapes=[pltpu.VMEM((TM_MOVE, D_MODEL), F32),
                            pltpu.VMEM((TM_MOVE, D_MODEL), F32),
                            pltpu.SemaphoreType.DMA(())]),
        out_shape=jax.ShapeDtypeStruct((t, D_MODEL), F32),
        compiler_params=_cparams(("arbitrary",)),
        name="moe_combine_ln",
    )(dest1, dest2, ys, x2d, meta, g.reshape(1, D_MODEL), b.reshape(1, D_MODEL))


def _moe_ln(x2d, router_w, w_gate, w_up, w_down, g, b):
    t = x2d.shape[0]
    meta, counts = _router(x2d, router_w)
    counts = counts[0, :N_EXPERTS].astype(jnp.int32)
    *visits, group_start = _expert_visits(counts, (2 * t) // TM_EXPERT)
    e1 = meta[:, R_E1].astype(jnp.int32)
    e2 = meta[:, R_E2].astype(jnp.int32)
    dest1 = group_start[e1] + meta[:, R_RANK1].astype(jnp.int32)
    dest2 = group_start[e2] + meta[:, R_RANK2].astype(jnp.int32)
    xs = _dispatch(x2d, dest1, dest2, 2 * t)
    ys = _expert_ffn(xs, tuple(visits), w_gate, w_up, w_down)
    return _combine_ln(ys, dest1, dest2, x2d, meta, g, b)


def _pack_w_in(w):
    dn = 4 * DN_WIDTH
    ab = w[:, dn:dn + 2 * DN_HEADS]
    pad = jnp.zeros((D_MODEL, LANES - 2 * DN_HEADS), w.dtype)
    packed = jnp.concatenate([w[:, :dn], w[:, dn + 2 * DN_HEADS:], ab, pad], axis=1).astype(BF16)
    return packed, ab.T.astype(BF16)


@jax.jit
def _trunk(x, positions, w_in, conv_w, a_log, dt_bias, dn_norm_w, sinks, w_out, ln_g, ln_b,
           ffn_w_gate, ffn_w_up, ffn_w_down, router_w, moe_w_gate, moe_w_up, moe_w_down):
    batch, seq, _ = x.shape
    x2d = x.reshape(batch * seq, D_MODEL)
    cos, sin = _rope_tables(positions)
    for layer in range(DEPTH):
        w_packed, wabt = _pack_w_in(w_in[layer])
        qkv, z, qsw, kvsw, ab, abt = _inproj(x2d, w_packed, wabt, cos, sin)
        o_dn = _deltanet(qkv, z, ab, abt, conv_w[layer], a_log[layer], dt_bias[layer],
                         dn_norm_w[layer], batch, seq)
        o_sw = _swa(qsw, kvsw, sinks[layer], batch, seq)
        x2d = _outproj_ln(o_dn, o_sw, w_out[layer].astype(BF16), x2d, ln_g[layer, 0], ln_b[layer, 0])
        i = layer // 2
        if layer % 2 == 0:
            x2d = _ffn_ln(x2d, ffn_w_gate[i], ffn_w_up[i], ffn_w_down[i], ln_g[layer, 1], ln_b[layer, 1])
        else:
            x2d = _moe_ln(x2d, router_w[i], moe_w_gate[i], moe_w_up[i], moe_w_down[i],
                          ln_g[layer, 1], ln_b[layer, 1])
    return x2d.reshape(batch, seq, D_MODEL)


def kernel(x, positions, w_in, conv_w, a_log, dt_bias, dn_norm_w, sinks, w_out, ln_g, ln_b,
           ffn_w_gate, ffn_w_up, ffn_w_down, router_w, moe_w_gate, moe_w_up, moe_w_down):
    return _trunk(x, positions, w_in, conv_w, a_log, dt_bias, dn_norm_w, sinks, w_out, ln_g, ln_b,
                  ffn_w_gate, ffn_w_up, ffn_w_down, router_w, moe_w_gate, moe_w_up, moe_w_down)
```

```python
import functools

import jax
import jax.numpy as jnp
import numpy as np
from jax import lax
from jax.experimental import pallas as pl
from jax.experimental.pallas import tpu as pltpu

F32 = jnp.float32
BF16 = jnp.bfloat16

D_MODEL = 1024
DEPTH = 4
DN_HEADS = 4
DN_HEAD_DIM = 128
DN_WIDTH = DN_HEADS * DN_HEAD_DIM
CONV_WIDTH = 4
CHUNK = 64
SW_Q_HEADS = 8
SW_KV_HEADS = 2
SW_GROUP = SW_Q_HEADS // SW_KV_HEADS
SW_HEAD_DIM = 64
SW_WIDTH = SW_Q_HEADS * SW_HEAD_DIM
SW_KV_WIDTH = SW_KV_HEADS * SW_HEAD_DIM
WINDOW = 128
ROPE_THETA = 10000.0
D_FF = 3584
N_EXPERTS = 8
ALPHA = (2.0 * DEPTH) ** 0.25
LN_EPS = 1e-5
RMS_EPS = 1e-6

LANES = 128
SUBLANES = 8
CONV_HALO = SUBLANES
VMEM_LIMIT = 56 * 1024 * 1024

C_QKV = (0, 3 * DN_WIDTH)
C_Z = (C_QKV[1], C_QKV[1] + DN_WIDTH)
C_QSW = (C_Z[1], C_Z[1] + SW_WIDTH)
C_KVSW = (C_QSW[1], C_QSW[1] + 2 * SW_KV_WIDTH)
C_AB = (C_KVSW[1], C_KVSW[1] + LANES)
IN_COLS_PACKED = C_AB[1]

TM_PROJ = 512
TM_FFN = 1024
TF_FFN = 512
TM_ROUTE = 1024
TM_EXPERT = 1024
TM_MOVE = 512
MOVE_UNROLL = 8


def _cparams(sem):
    return pltpu.CompilerParams(dimension_semantics=sem, vmem_limit_bytes=VMEM_LIMIT)


def _bdot(a, b):
    return jnp.dot(a.astype(BF16), b.astype(BF16), preferred_element_type=F32)


def _bdot_nt(a, b):
    return lax.dot_general(a.astype(BF16), b.astype(BF16), (((1,), (1,)), ((), ())),
                           preferred_element_type=F32)


def _bdot_tn(a, b):
    return lax.dot_general(a.astype(BF16), b.astype(BF16), (((0,), (0,)), ((), ())),
                           preferred_element_type=F32)


def _silu(x):
    return x * jax.nn.sigmoid(x)


def _softplus(x):
    return jnp.maximum(x, 0.0) + jnp.log1p(jnp.exp(-jnp.abs(x)))


def _layer_norm(y, g, b):
    mu = jnp.mean(y, axis=-1, keepdims=True)
    d = y - mu
    var = jnp.mean(d * d, axis=-1, keepdims=True)
    return d * lax.rsqrt(var + LN_EPS) * g + b


def _rope_table_kernel(pos_ref, invf_ref, sign_ref, cos_ref, sin_ref):
    ang = pos_ref[...].astype(F32) * invf_ref[...]
    cos_ref[...] = jnp.cos(ang)
    sin_ref[...] = jnp.sin(ang) * sign_ref[...]


def _rope_tables(positions):
    t = positions.size
    half = SW_HEAD_DIM // 2
    inv_freq = ROPE_THETA ** (-jnp.arange(0, SW_HEAD_DIM, 2, dtype=F32) / SW_HEAD_DIM)
    reps = LANES // half
    invf = jnp.tile(inv_freq, reps).reshape(1, LANES)
    sign = jnp.tile(jnp.concatenate([-jnp.ones((half,), F32), jnp.ones((half,), F32)]),
                    LANES // SW_HEAD_DIM).reshape(1, LANES)
    tm = min(TM_PROJ, t)
    row = pl.BlockSpec((1, LANES), lambda i: (0, 0))
    tab = pl.BlockSpec((tm, LANES), lambda i: (i, 0))
    return pl.pallas_call(
        _rope_table_kernel,
        grid=(t // tm,),
        in_specs=[pl.BlockSpec((tm, 1), lambda i: (i, 0)), row, row],
        out_specs=[tab, tab],
        out_shape=[jax.ShapeDtypeStruct((t, LANES), F32)] * 2,
        compiler_params=_cparams(("parallel",)),
        name="rope_tables",
    )(positions.reshape(t, 1), invf, sign)


def _rope(x, cos, sin_signed):
    width = x.shape[-1]
    half = SW_HEAD_DIM // 2
    lane = lax.broadcasted_iota(jnp.int32, x.shape, 1)
    first = (lane % SW_HEAD_DIM) < half
    partner = jnp.where(first, pltpu.roll(x, width - half, 1), pltpu.roll(x, half, 1))
    return x * cos + partner * sin_signed


def _inproj_kernel(x_ref, w_ref, wabt_ref, cos_ref, sin_ref,
                   qkv_ref, z_ref, qsw_ref, kvsw_ref, ab_ref, abt_ref):
    xb = x_ref[...].astype(BF16)

    def mm(cols):
        return jnp.dot(xb, w_ref[0, :, cols[0]:cols[1]], preferred_element_type=F32)

    qkv_ref[...] = mm(C_QKV).astype(BF16)
    z_ref[...] = mm(C_Z).astype(BF16)
    cos = cos_ref[...]
    sin = sin_ref[...]
    reps = SW_WIDTH // LANES
    q = _rope(mm(C_QSW), jnp.tile(cos, (1, reps)), jnp.tile(sin, (1, reps))) * (SW_HEAD_DIM ** -0.5)
    qsw_ref[...] = q.astype(BF16)
    kv = mm(C_KVSW)
    k = _rope(kv[:, :SW_KV_WIDTH], cos, sin)
    kvsw_ref[...] = jnp.concatenate([k, kv[:, SW_KV_WIDTH:]], axis=1).astype(BF16)
    ab_ref[...] = mm(C_AB)
    abt_ref[...] = lax.dot_general(wabt_ref[0], xb, (((1,), (1,)), ((), ())),
                                   preferred_element_type=F32)


def _inproj(x2d, w_packed, wabt, layer, cos, sin):
    t = x2d.shape[0]
    tm = min(TM_PROJ, t)

    def rows(width):
        return pl.BlockSpec((tm, width), lambda i: (i, 0))

    return pl.pallas_call(
        _inproj_kernel,
        grid=(t // tm,),
        in_specs=[rows(D_MODEL),
                  pl.BlockSpec((1, D_MODEL, IN_COLS_PACKED), lambda i: (layer, 0, 0)),
                  pl.BlockSpec((1, SUBLANES, D_MODEL), lambda i: (layer, 0, 0)),
                  rows(LANES), rows(LANES)],
        out_specs=[rows(3 * DN_WIDTH), rows(DN_WIDTH), rows(SW_WIDTH), rows(2 * SW_KV_WIDTH),
                   rows(LANES), pl.BlockSpec((SUBLANES, tm), lambda i: (0, i))],
        out_shape=[jax.ShapeDtypeStruct((t, 3 * DN_WIDTH), BF16),
                   jax.ShapeDtypeStruct((t, DN_WIDTH), BF16),
                   jax.ShapeDtypeStruct((t, SW_WIDTH), BF16),
                   jax.ShapeDtypeStruct((t, 2 * SW_KV_WIDTH), BF16),
                   jax.ShapeDtypeStruct((t, LANES), F32),
                   jax.ShapeDtypeStruct((SUBLANES, t), F32)],
        compiler_params=_cparams(("parallel",)),
        name="in_proj",
    )(x2d, w_packed, wabt, cos, sin)


def _unit_lower_inverse(lmat, row, col):
    same16 = (row // 16) == (col // 16)
    same32 = (row // 32) == (col // 32)
    eye = (row == col).astype(F32)
    ld = jnp.where(same16, lmat, 0.0)
    inv = eye - ld
    power = ld
    for _ in range(3):
        power = _bdot(power, power)
        inv = inv + _bdot(inv, power)
    for off_diag in (same32 & ~same16, ~same32):
        c = jnp.where(off_diag, lmat, 0.0)
        inv = inv - _bdot(_bdot(inv, c), inv)
    return inv


def _deltanet_kernel(qkv_ref, z_ref, ab_ref, abt_ref, convw_ref, prow_ref, pcol_ref, normw_ref,
                     o_ref, xp_ref, state_ref):
    @pl.when(pl.program_id(1) == 0)
    def _():
        xp_ref[0:CONV_HALO, :] = jnp.zeros((CONV_HALO, 3 * DN_WIDTH), F32)
        state_ref[...] = jnp.zeros_like(state_ref)

    xp_ref[CONV_HALO:CONV_HALO + CHUNK, :] = qkv_ref[...].astype(F32)
    convw = convw_ref[...]
    y = None
    for tap in range(CONV_WIDTH):
        start = CONV_HALO - (CONV_WIDTH - 1) + tap
        term = xp_ref[start:start + CHUNK, :] * convw[tap:tap + 1, :]
        y = term if y is None else y + term
    xp_ref[0:CONV_HALO, :] = xp_ref[CHUNK:CHUNK + CONV_HALO, :]
    qkv = _silu(y)

    row = lax.broadcasted_iota(jnp.int32, (CHUNK, CHUNK), 0)
    col = lax.broadcasted_iota(jnp.int32, (CHUNK, CHUNK), 1)
    lower_incl = row >= col
    strict_lower = row > col
    tri = lower_incl.astype(F32)
    tri_t = (row <= col).astype(F32)

    ab = ab_ref[...]
    g_cols = -jnp.exp(prow_ref[0:1, :]) * _softplus(ab + prow_ref[1:2, :])
    abt = abt_ref[0]
    g_rows = -jnp.exp(pcol_ref[:, 0:1]) * _softplus(abt + pcol_ref[:, 1:2])
    gcum_cols = jnp.dot(tri, g_cols, preferred_element_type=F32, precision=lax.Precision.HIGHEST)
    gcum_rows = jnp.dot(g_rows, tri_t, preferred_element_type=F32, precision=lax.Precision.HIGHEST)
    beta_cols = jax.nn.sigmoid(ab)

    normw = normw_ref[...]
    outs = []
    for h in range(DN_HEADS):
        sl = slice(h * DN_HEAD_DIM, (h + 1) * DN_HEAD_DIM)
        q = qkv[:, sl]
        k = qkv[:, DN_WIDTH + h * DN_HEAD_DIM:DN_WIDTH + (h + 1) * DN_HEAD_DIM]
        v = qkv[:, 2 * DN_WIDTH + h * DN_HEAD_DIM:2 * DN_WIDTH + (h + 1) * DN_HEAD_DIM]
        q = q * lax.rsqrt(jnp.sum(q * q, axis=-1, keepdims=True) + 1e-6) * (DN_HEAD_DIM ** -0.5)
        k = k * lax.rsqrt(jnp.sum(k * k, axis=-1, keepdims=True) + 1e-6)
        gc = gcum_cols[:, h:h + 1]
        gr = gcum_rows[h:h + 1, :]
        g_last = gcum_cols[CHUNK - 1:CHUNK, h:h + 1]
        beta = beta_cols[:, DN_HEADS + h:DN_HEADS + h + 1]
        decay = jnp.where(lower_incl, jnp.exp(jnp.where(lower_incl, gc - gr, 0.0)), 0.0)
        k_beta = k * beta
        lmat = jnp.where(strict_lower, _bdot_nt(k_beta, k) * decay, 0.0)
        tinv = _unit_lower_inverse(lmat, row, col)
        exp_gc = jnp.exp(gc)
        u = _bdot(tinv, v * beta)
        w = _bdot(tinv, k_beta * exp_gc)
        attn = jnp.where(lower_incl, _bdot_nt(q, k) * decay, 0.0)
        q_dec = q * exp_gc
        k_end = k * jnp.exp(g_last - gc)
        state = state_ref[h]
        v_new = u - _bdot(w, state)
        o = _bdot(q_dec, state) + _bdot(attn, v_new)
        state_ref[h] = state * jnp.exp(g_last) + _bdot_tn(k_end, v_new)
        o = o * lax.rsqrt(jnp.mean(o * o, axis=-1, keepdims=True) + RMS_EPS)
        outs.append(o * normw * _silu(z_ref[:, sl].astype(F32)))
    o_ref[...] = jnp.concatenate(outs, axis=1).astype(o_ref.dtype)


def _deltanet(qkv, z, ab, abt, conv_w, a_log, dt_bias, dn_norm_w, batch, seq):
    t = batch * seq
    nchunk = seq // CHUNK
    abt3 = abt.reshape(SUBLANES, t // CHUNK, CHUNK).transpose(1, 0, 2)
    prow = jnp.zeros((SUBLANES, LANES), F32).at[0, :DN_HEADS].set(a_log).at[1, :DN_HEADS].set(dt_bias)
    pcol = (jnp.zeros((SUBLANES, LANES), F32)
            .at[:DN_HEADS, 0].set(a_log).at[DN_HEADS:2 * DN_HEADS, 0].set(a_log)
            .at[:DN_HEADS, 1].set(dt_bias).at[DN_HEADS:2 * DN_HEADS, 1].set(dt_bias))

    def rows(width):
        return pl.BlockSpec((CHUNK, width), lambda b, c: (b * nchunk + c, 0))

    def const(shape):
        return pl.BlockSpec(shape, lambda b, c: (0,) * len(shape))

    return pl.pallas_call(
        _deltanet_kernel,
        grid=(batch, nchunk),
        in_specs=[rows(3 * DN_WIDTH), rows(DN_WIDTH), rows(LANES),
                  pl.BlockSpec((1, SUBLANES, CHUNK), lambda b, c: (b * nchunk + c, 0, 0)),
                  const((CONV_WIDTH, 3 * DN_WIDTH)), const((SUBLANES, LANES)),
                  const((SUBLANES, LANES)), const((1, DN_HEAD_DIM))],
        out_specs=rows(DN_WIDTH),
        out_shape=jax.ShapeDtypeStruct((t, DN_WIDTH), BF16),
        scratch_shapes=[pltpu.VMEM((CONV_HALO + CHUNK, 3 * DN_WIDTH), F32),
                        pltpu.VMEM((DN_HEADS, DN_HEAD_DIM, DN_HEAD_DIM), F32)],
        compiler_params=_cparams(("parallel", "arbitrary")),
        name="gated_deltanet",
    )(qkv, z, ab, abt3, conv_w, prow, pcol, dn_norm_w.reshape(1, DN_HEAD_DIM))


def _swa_kernel(q_ref, kvc_ref, kvp_ref, sink_ref, o_ref):
    blk = pl.program_id(1)
    qi = lax.broadcasted_iota(jnp.int32, (WINDOW, WINDOW), 0)
    kj = lax.broadcasted_iota(jnp.int32, (WINDOW, WINDOW), 1)
    valid_prev = (kj > qi) & (blk > 0)
    valid_cur = kj <= qi
    outs = []
    for hq in range(SW_Q_HEADS):
        hk = hq // SW_GROUP
        ks = slice(hk * SW_HEAD_DIM, (hk + 1) * SW_HEAD_DIM)
        vs = slice(SW_KV_WIDTH + hk * SW_HEAD_DIM, SW_KV_WIDTH + (hk + 1) * SW_HEAD_DIM)
        q = q_ref[:, hq * SW_HEAD_DIM:(hq + 1) * SW_HEAD_DIM]
        s_prev = jnp.where(valid_prev, _bdot_nt(q, kvp_ref[:, ks]), -jnp.inf)
        s_cur = jnp.where(valid_cur, _bdot_nt(q, kvc_ref[:, ks]), -jnp.inf)
        sink = sink_ref[0:1, hq:hq + 1]
        m = jnp.maximum(jnp.maximum(jnp.max(s_prev, axis=-1, keepdims=True),
                                    jnp.max(s_cur, axis=-1, keepdims=True)), sink)
        p_prev = jnp.exp(s_prev - m)
        p_cur = jnp.exp(s_cur - m)
        denom = (jnp.sum(p_prev, axis=-1, keepdims=True) + jnp.sum(p_cur, axis=-1, keepdims=True)
                 + jnp.exp(sink - m))
        outs.append((_bdot(p_prev, kvp_ref[:, vs]) + _bdot(p_cur, kvc_ref[:, vs])) / denom)
    o_ref[...] = jnp.concatenate(outs, axis=1).astype(o_ref.dtype)


def _swa(qsw, kvsw, sinks, batch, seq):
    t = batch * seq
    nblk = seq // WINDOW
    return pl.pallas_call(
        _swa_kernel,
        grid=(batch, nblk),
        in_specs=[pl.BlockSpec((WINDOW, SW_WIDTH), lambda b, i: (b * nblk + i, 0)),
                  pl.BlockSpec((WINDOW, 2 * SW_KV_WIDTH), lambda b, i: (b * nblk + i, 0)),
                  pl.BlockSpec((WINDOW, 2 * SW_KV_WIDTH),
                               lambda b, i: (b * nblk + jnp.maximum(i - 1, 0), 0)),
                  pl.BlockSpec((1, SW_Q_HEADS), lambda b, i: (0, 0))],
        out_specs=pl.BlockSpec((WINDOW, SW_WIDTH), lambda b, i: (b * nblk + i, 0)),
        out_shape=jax.ShapeDtypeStruct((t, SW_WIDTH), BF16),
        compiler_params=_cparams(("parallel", "parallel")),
        name="sliding_window_attn",
    )(qsw, kvsw, kvsw, sinks.reshape(1, SW_Q_HEADS))


def _outproj_kernel(odn_ref, osw_ref, w_ref, x_ref, g_ref, b_ref, o_ref):
    mix = (jnp.dot(odn_ref[...], w_ref[0, 0:DN_WIDTH, :], preferred_element_type=F32)
           + jnp.dot(osw_ref[...], w_ref[0, DN_WIDTH:, :], preferred_element_type=F32))
    o_ref[...] = _layer_norm(ALPHA * x_ref[...] + mix, g_ref[...], b_ref[...])


def _outproj_ln(o_dn, o_sw, w_out_bf16, layer, x2d, g, b):
    t = x2d.shape[0]
    tm = min(TM_PROJ, t)

    def rows(width):
        return pl.BlockSpec((tm, width), lambda i: (i, 0))

    vec = pl.BlockSpec((1, D_MODEL), lambda i: (0, 0))
    return pl.pallas_call(
        _outproj_kernel,
        grid=(t // tm,),
        in_specs=[rows(DN_WIDTH), rows(SW_WIDTH),
                  pl.BlockSpec((1, DN_WIDTH + SW_WIDTH, D_MODEL), lambda i: (layer, 0, 0)),
                  rows(D_MODEL), vec, vec],
        out_specs=rows(D_MODEL),
        out_shape=jax.ShapeDtypeStruct((t, D_MODEL), F32),
        compiler_params=_cparams(("parallel",)),
        name="out_proj_ln",
    )(o_dn, o_sw, w_out_bf16, x2d, g.reshape(1, D_MODEL), b.reshape(1, D_MODEL))


def _ffn_kernel(x_ref, wg_ref, wu_ref, wd_ref, g_ref, b_ref, o_ref, xb_ref, acc_ref):
    f = pl.program_id(1)

    @pl.when(f == 0)
    def _():
        xb_ref[...] = x_ref[...].astype(BF16)
        acc_ref[...] = jnp.zeros_like(acc_ref)

    xb = xb_ref[...]
    gate = jnp.dot(xb, wg_ref[0].astype(BF16), preferred_element_type=F32)
    up = jnp.dot(xb, wu_ref[0].astype(BF16), preferred_element_type=F32)
    hid = (_silu(gate) * up).astype(BF16)
    acc_ref[...] += jnp.dot(hid, wd_ref[0].astype(BF16), preferred_element_type=F32)

    @pl.when(f == pl.num_programs(1) - 1)
    def _():
        o_ref[...] = _layer_norm(ALPHA * x_ref[...] + acc_ref[...], g_ref[...], b_ref[...])


def _ffn_ln(x2d, w_gate, w_up, w_down, idx, g, b):
    t = x2d.shape[0]
    tm = min(TM_FFN, t)
    vec = pl.BlockSpec((1, D_MODEL), lambda i, f: (0, 0))
    return pl.pallas_call(
        _ffn_kernel,
        grid=(t // tm, D_FF // TF_FFN),
        in_specs=[pl.BlockSpec((tm, D_MODEL), lambda i, f: (i, 0)),
                  pl.BlockSpec((1, D_MODEL, TF_FFN), lambda i, f: (idx, 0, f)),
                  pl.BlockSpec((1, D_MODEL, TF_FFN), lambda i, f: (idx, 0, f)),
                  pl.BlockSpec((1, TF_FFN, D_MODEL), lambda i, f: (idx, f, 0)),
                  vec, vec],
        out_specs=pl.BlockSpec((tm, D_MODEL), lambda i, f: (i, 0)),
        out_shape=jax.ShapeDtypeStruct((t, D_MODEL), F32),
        scratch_shapes=[pltpu.VMEM((tm, D_MODEL), BF16), pltpu.VMEM((tm, D_MODEL), F32)],
        compiler_params=_cparams(("parallel", "arbitrary")),
        name="dense_swiglu_ln",
    )(x2d, w_gate, w_up, w_down, g.reshape(1, D_MODEL), b.reshape(1, D_MODEL))


R_E1, R_E2, R_RANK1, R_RANK2, R_G1, R_G2 = range(6)


def _router_kernel(x_ref, wr_ref, meta_ref, count_ref, carry_ref):
    i = pl.program_id(0)

    @pl.when(i == 0)
    def _():
        carry_ref[...] = jnp.zeros_like(carry_ref)

    tm = x_ref.shape[0]
    logits = jnp.dot(x_ref[...], wr_ref[...], preferred_element_type=F32,
                     precision=lax.Precision.HIGHEST)
    lane = lax.broadcasted_iota(jnp.int32, logits.shape, 1)
    logits = jnp.where(lane < N_EXPERTS, logits, -jnp.inf)
    m1 = jnp.max(logits, axis=-1, keepdims=True)
    e1 = jnp.min(jnp.where(logits == m1, lane, LANES), axis=-1, keepdims=True)
    rest = jnp.where(lane == e1, -jnp.inf, logits)
    m2 = jnp.max(rest, axis=-1, keepdims=True)
    e2 = jnp.min(jnp.where(rest == m2, lane, LANES), axis=-1, keepdims=True)
    ex = jnp.exp(m2 - m1)
    g1 = 1.0 / (1.0 + ex)
    g2 = ex / (1.0 + ex)

    hit = ((lane == e1) | (lane == e2)).astype(BF16)
    r = lax.broadcasted_iota(jnp.int32, (tm, tm), 0)
    c = lax.broadcasted_iota(jnp.int32, (tm, tm), 1)
    before = jnp.dot((c < r).astype(BF16), hit, preferred_element_type=F32) + carry_ref[0:1, :]
    rank1 = jnp.sum(jnp.where(lane == e1, before, 0.0), axis=-1, keepdims=True)
    rank2 = jnp.sum(jnp.where(lane == e2, before, 0.0), axis=-1, keepdims=True)
    carry_ref[0:1, :] = carry_ref[0:1, :] + jnp.sum(hit.astype(F32), axis=0, keepdims=True)
    count_ref[...] = jnp.broadcast_to(carry_ref[0:1, :], count_ref.shape)

    meta = jnp.zeros((tm, SUBLANES), F32)
    lane8 = lax.broadcasted_iota(jnp.int32, (tm, SUBLANES), 1)
    for idx, val in ((R_E1, e1.astype(F32)), (R_E2, e2.astype(F32)), (R_RANK1, rank1),
                     (R_RANK2, rank2), (R_G1, g1), (R_G2, g2)):
        meta = jnp.where(lane8 == idx, val, meta)
    meta_ref[...] = meta


def _router(x2d, router_w):
    t = x2d.shape[0]
    tm = min(TM_ROUTE, t)
    wr = jnp.zeros((D_MODEL, LANES), F32).at[:, :N_EXPERTS].set(router_w)
    return pl.pallas_call(
        _router_kernel,
        grid=(t // tm,),
        in_specs=[pl.BlockSpec((tm, D_MODEL), lambda i: (i, 0)),
                  pl.BlockSpec((D_MODEL, LANES), lambda i: (0, 0))],
        out_specs=[pl.BlockSpec((tm, SUBLANES), lambda i: (i, 0)),
                   pl.BlockSpec((SUBLANES, LANES), lambda i: (0, 0))],
        out_shape=[jax.ShapeDtypeStruct((t, SUBLANES), F32),
                   jax.ShapeDtypeStruct((SUBLANES, LANES), F32)],
        scratch_shapes=[pltpu.VMEM((SUBLANES, LANES), F32)],
        compiler_params=_cparams(("arbitrary",)),
        name="moe_router",
    )(x2d, wr)


def _row_copy(src_ref, src_row, dst_ref, dst_row, sem):
    return pltpu.make_async_copy(src_ref.at[pl.ds(src_row, 1), :], dst_ref.at[pl.ds(dst_row, 1), :], sem)


def _dispatch_kernel(dest1_ref, dest2_ref, x_ref, xs_ref, sem):
    base = pl.program_id(0) * TM_MOVE

    def issue(j, carry):
        tok = base + j
        _row_copy(x_ref, j, xs_ref, dest1_ref[tok], sem).start(priority=0)
        _row_copy(x_ref, j, xs_ref, dest2_ref[tok], sem).start(priority=1)
        return carry

    lax.fori_loop(0, TM_MOVE, issue, 0, unroll=MOVE_UNROLL)
    for _ in range(2):
        pltpu.make_async_copy(x_ref, xs_ref.at[pl.ds(0, TM_MOVE), :], sem).wait()


def _dispatch(x2d, dest1, dest2, rows_sorted):
    t = x2d.shape[0]
    return pl.pallas_call(
        _dispatch_kernel,
        grid_spec=pltpu.PrefetchScalarGridSpec(
            num_scalar_prefetch=2,
            grid=(t // TM_MOVE,),
            in_specs=[pl.BlockSpec((TM_MOVE, D_MODEL), lambda i, d1, d2: (i, 0))],
            out_specs=pl.BlockSpec(memory_space=pl.ANY),
            scratch_shapes=[pltpu.SemaphoreType.DMA(())]),
        out_shape=jax.ShapeDtypeStruct((rows_sorted, D_MODEL), F32),
        compiler_params=_cparams(("arbitrary",)),
        name="moe_dispatch",
    )(dest1, dest2, x2d)


def _expert_kernel(vtile_ref, vexpert_ref, vfidx_ref, vlo_ref, vhi_ref, vfirst_ref, vlast_ref,
                   xs_ref, wg_ref, wu_ref, wd_ref, ys_ref, xb_ref, acc_ref):
    v = pl.program_id(0)
    f = pl.program_id(1)
    lo = vlo_ref[v]
    hi = vhi_ref[v]

    @pl.when(hi > lo)
    def _():
        @pl.when(f == 0)
        def _():
            rowid = lax.broadcasted_iota(jnp.int32, xs_ref.shape, 0)
            xb_ref[...] = jnp.where((rowid >= lo) & (rowid < hi), xs_ref[...], 0.0).astype(BF16)

        @pl.when((f == 0) & (vfirst_ref[v] == 1))
        def _():
            acc_ref[...] = jnp.zeros_like(acc_ref)

        xb = xb_ref[...]
        gate = jnp.dot(xb, wg_ref[0, 0].astype(BF16), preferred_element_type=F32)
        up = jnp.dot(xb, wu_ref[0, 0].astype(BF16), preferred_element_type=F32)
        hid = (_silu(gate) * up).astype(BF16)
        acc_ref[...] += jnp.dot(hid, wd_ref[0, 0].astype(BF16), preferred_element_type=F32)

        @pl.when((f == pl.num_programs(1) - 1) & (vlast_ref[v] == 1))
        def _():
            ys_ref[...] = acc_ref[...]


def _expert_visits(counts, n_tiles):
    n_visits = n_tiles + N_EXPERTS - 1
    ends = jnp.cumsum(counts)
    offs = ends - counts
    first_tile = offs // TM_EXPERT
    per_expert = jnp.where(counts > 0, (ends - 1) // TM_EXPERT - first_tile + 1, 0)
    vend = jnp.cumsum(per_expert)
    vstart = vend - per_expert
    total = vend[-1]
    vid = jnp.arange(n_visits, dtype=jnp.int32)
    real = vid < total
    vid_c = jnp.minimum(vid, total - 1)
    vexpert = jnp.minimum(jnp.sum(vid_c[:, None] >= vend[None, :], axis=1), N_EXPERTS - 1)
    vtile = first_tile[vexpert] + vid_c - vstart[vexpert]
    lo = jnp.maximum(offs[vexpert], vtile * TM_EXPERT) - vtile * TM_EXPERT
    hi = jnp.minimum(ends[vexpert], (vtile + 1) * TM_EXPERT) - vtile * TM_EXPERT
    prev_tile = jnp.concatenate([jnp.full((1,), -1, vtile.dtype), vtile[:-1]])
    next_tile = jnp.concatenate([vtile[1:], jnp.full((1,), -1, vtile.dtype)])
    first = real & (vtile != prev_tile)
    last = real & ((vtile != next_tile) | (vid == total - 1))
    i32 = lambda a: a.astype(jnp.int32)
    fpin = jnp.where(real, -1, D_FF // TF_FFN - 1)
    return (i32(vtile), i32(vexpert), i32(fpin), i32(jnp.where(real, lo, 0)),
            i32(jnp.where(real, hi, 0)), i32(first), i32(last), i32(offs))


def _expert_ffn(xs, visits, w_gate, w_up, w_down, idx):
    rows_sorted = xs.shape[0]
    n_visits = visits[0].shape[0]
    nf = D_FF // TF_FFN

    def f_idx(v, f, vf):
        return jnp.where(vf[v] >= 0, vf[v], f)

    def rows_map(v, f, vt, ve, vf, *_):
        return (vt[v], 0)

    def wcol_map(v, f, vt, ve, vf, *_):
        return (idx, ve[v], 0, f_idx(v, f, vf))

    def wrow_map(v, f, vt, ve, vf, *_):
        return (idx, ve[v], f_idx(v, f, vf), 0)

    return pl.pallas_call(
        _expert_kernel,
        grid_spec=pltpu.PrefetchScalarGridSpec(
            num_scalar_prefetch=len(visits),
            grid=(n_visits, nf),
            in_specs=[pl.BlockSpec((TM_EXPERT, D_MODEL), rows_map),
                      pl.BlockSpec((1, 1, D_MODEL, TF_FFN), wcol_map),
                      pl.BlockSpec((1, 1, D_MODEL, TF_FFN), wcol_map),
                      pl.BlockSpec((1, 1, TF_FFN, D_MODEL), wrow_map)],
            out_specs=pl.BlockSpec((TM_EXPERT, D_MODEL), rows_map),
            scratch_shapes=[pltpu.VMEM((TM_EXPERT, D_MODEL), BF16),
                            pltpu.VMEM((TM_EXPERT, D_MODEL), F32)]),
        out_shape=jax.ShapeDtypeStruct((rows_sorted, D_MODEL), F32),
        compiler_params=_cparams(("arbitrary", "arbitrary")),
        name="moe_expert_swiglu",
    )(*visits, xs, w_gate, w_up, w_down)


def _combine_kernel(dest1_ref, dest2_ref, ys_ref, x_ref, meta_ref, g_ref, b_ref, o_ref,
                    buf1_ref, buf2_ref, sem):
    base = pl.program_id(0) * TM_MOVE

    def issue(j, carry):
        tok = base + j
        _row_copy(ys_ref, dest1_ref[tok], buf1_ref, j, sem).start(priority=0)
        _row_copy(ys_ref, dest2_ref[tok], buf2_ref, j, sem).start(priority=1)
        return carry

    lax.fori_loop(0, TM_MOVE, issue, 0, unroll=MOVE_UNROLL)
    for buf_ref in (buf1_ref, buf2_ref):
        pltpu.make_async_copy(ys_ref.at[pl.ds(0, TM_MOVE), :], buf_ref, sem).wait()

    meta = meta_ref[...]
    g1 = meta[:, R_G1:R_G1 + 1]
    g2 = meta[:, R_G2:R_G2 + 1]
    moe = g1 * buf1_ref[...] + g2 * buf2_ref[...]
    o_ref[...] = _layer_norm(ALPHA * x_ref[...] + moe, g_ref[...], b_ref[...])


def _combine_ln(ys, dest1, dest2, x2d, meta, g, b):
    t = x2d.shape[0]
    vec = pl.BlockSpec((1, D_MODEL), lambda i, d1, d2: (0, 0))
    return pl.pallas_call(
        _combine_kernel,
        grid_spec=pltpu.PrefetchScalarGridSpec(
            num_scalar_prefetch=2,
            grid=(t // TM_MOVE,),
            in_specs=[pl.BlockSpec(memory_space=pl.ANY),
                      pl.BlockSpec((TM_MOVE, D_MODEL), lambda i, d1, d2: (i, 0)),
                      pl.BlockSpec((TM_MOVE, SUBLANES), lambda i, d1, d2: (i, 0)),
                      vec, vec],
            out_specs=pl.BlockSpec((TM_MOVE, D_MODEL), lambda i, d1, d2: (i, 0)),
            scratch_shapes=[pltpu.VMEM((TM_MOVE, D_MODEL), F32),
                            pltpu.VMEM((TM_MOVE, D_MODEL), F32),
                            pltpu.SemaphoreType.DMA(())]),
        out_shape=jax.ShapeDtypeStruct((t, D_MODEL), F32),
        compiler_params=_cparams(("arbitrary",)),
        name="moe_combine_ln",
    )(dest1, dest2, ys, x2d, meta, g.reshape(1, D_MODEL), b.reshape(1, D_MODEL))


def _moe_ln(x2d, router_w, w_gate, w_up, w_down, idx, g, b):
    t = x2d.shape[0]
    meta, counts = _router(x2d, router_w)
    counts = counts[0, :N_EXPERTS].astype(jnp.int32)
    *visits, group_start = _expert_visits(counts, (2 * t) // TM_EXPERT)
    e1 = meta[:, R_E1].astype(jnp.int32)
    e2 = meta[:, R_E2].astype(jnp.int32)
    dest1 = group_start[e1] + meta[:, R_RANK1].astype(jnp.int32)
    dest2 = group_start[e2] + meta[:, R_RANK2].astype(jnp.int32)
    xs = _dispatch(x2d, dest1, dest2, 2 * t)
    ys = _expert_ffn(xs, tuple(visits), w_gate, w_up, w_down, idx)
    return _combine_ln(ys, dest1, dest2, x2d, meta, g, b)


def _pack_w_in(w):
    dn = 4 * DN_WIDTH
    ab = w[..., dn:dn + 2 * DN_HEADS]
    pad = jnp.zeros(w.shape[:-1] + (LANES - 2 * DN_HEADS,), w.dtype)
    packed = jnp.concatenate([w[..., :dn], w[..., dn + 2 * DN_HEADS:], ab, pad], axis=-1).astype(BF16)
    return packed, jnp.swapaxes(ab, -1, -2).astype(BF16)


@jax.jit
def _trunk(x, positions, w_in, conv_w, a_log, dt_bias, dn_norm_w, sinks, w_out, ln_g, ln_b,
           ffn_w_gate, ffn_w_up, ffn_w_down, router_w, moe_w_gate, moe_w_up, moe_w_down):
    batch, seq, _ = x.shape
    x2d = x.reshape(batch * seq, D_MODEL)
    cos, sin = _rope_tables(positions)
    w_packed, wabt = _pack_w_in(w_in)
    w_out_bf16 = w_out.astype(BF16)
    for layer in range(DEPTH):
        qkv, z, qsw, kvsw, ab, abt = _inproj(x2d, w_packed, wabt, layer, cos, sin)
        o_dn = _deltanet(qkv, z, ab, abt, conv_w[layer], a_log[layer], dt_bias[layer],
                         dn_norm_w[layer], batch, seq)
        o_sw = _swa(qsw, kvsw, sinks[layer], batch, seq)
        x2d = _outproj_ln(o_dn, o_sw, w_out_bf16, layer, x2d, ln_g[layer, 0], ln_b[layer, 0])
        i = layer // 2
        if layer % 2 == 0:
            x2d = _ffn_ln(x2d, ffn_w_gate, ffn_w_up, ffn_w_down, i, ln_g[layer, 1], ln_b[layer, 1])
        else:
            x2d = _moe_ln(x2d, router_w[i], moe_w_gate, moe_w_up, moe_w_down, i,
                          ln_g[layer, 1], ln_b[layer, 1])
    return x2d.reshape(batch, seq, D_MODEL)


def kernel(x, positions, w_in, conv_w, a_log, dt_bias, dn_norm_w, sinks, w_out, ln_g, ln_b,
           ffn_w_gate, ffn_w_up, ffn_w_down, router_w, moe_w_gate, moe_w_up, moe_w_down):
    return _trunk(x, positions, w_in, conv_w, a_log, dt_bias, dn_norm_w, sinks, w_out, ln_g, ln_b,
                  ffn_w_gate, ffn_w_up, ffn_w_down, router_w, moe_w_gate, moe_w_up, moe_w_down)
```

```python
import functools

import jax
import jax.numpy as jnp
import numpy as np
from jax import lax
from jax.experimental import pallas as pl
from jax.experimental.pallas import tpu as pltpu

F32 = jnp.float32
BF16 = jnp.bfloat16

D_MODEL = 1024
DEPTH = 4
DN_HEADS = 4
DN_HEAD_DIM = 128
DN_WIDTH = DN_HEADS * DN_HEAD_DIM
CONV_WIDTH = 4
CHUNK = 64
SW_Q_HEADS = 8
SW_KV_HEADS = 2
SW_GROUP = SW_Q_HEADS // SW_KV_HEADS
SW_HEAD_DIM = 64
SW_WIDTH = SW_Q_HEADS * SW_HEAD_DIM
SW_KV_WIDTH = SW_KV_HEADS * SW_HEAD_DIM
WINDOW = 128
ROPE_THETA = 10000.0
D_FF = 3584
N_EXPERTS = 8
ALPHA = (2.0 * DEPTH) ** 0.25
LN_EPS = 1e-5
RMS_EPS = 1e-6

LANES = 128
SUBLANES = 8
DN_HALO = 16
DN_BLOCK = 512
DN_SUB = 128
DN_SCAN = 128
SWA_BLOCKS = 4
VMEM_LIMIT = 56 * 1024 * 1024

C_QKV = (0, 3 * DN_WIDTH)
C_Z = (C_QKV[1], C_QKV[1] + DN_WIDTH)
C_QSW = (C_Z[1], C_Z[1] + SW_WIDTH)
C_KVSW = (C_QSW[1], C_QSW[1] + 2 * SW_KV_WIDTH)
C_AB = (C_KVSW[1], C_KVSW[1] + LANES)
IN_COLS_PACKED = C_AB[1]

TM_PROJ = 512
TM_FFN = 1024
TF_FFN = 512
TM_ROUTE = 1024
TM_EXPERT = 1024
TM_MOVE = 512
MOVE_UNROLL = 8


def _cparams(sem):
    return pltpu.CompilerParams(dimension_semantics=sem, vmem_limit_bytes=VMEM_LIMIT)


def _bdot(a, b):
    return jnp.dot(a.astype(BF16), b.astype(BF16), preferred_element_type=F32)


def _bdot_nt(a, b):
    return lax.dot_general(a.astype(BF16), b.astype(BF16), (((1,), (1,)), ((), ())),
                           preferred_element_type=F32)


def _silu(x):
    return x * jax.nn.sigmoid(x)


def _softplus(x):
    return jnp.maximum(x, 0.0) + jnp.log1p(jnp.exp(-jnp.abs(x)))


def _layer_norm(y, g, b):
    mu = jnp.mean(y, axis=-1, keepdims=True)
    d = y - mu
    var = jnp.mean(d * d, axis=-1, keepdims=True)
    return d * lax.rsqrt(var + LN_EPS) * g + b


def _rope_table_kernel(pos_ref, invf_ref, sign_ref, cos_ref, sin_ref):
    ang = pos_ref[...].astype(F32) * invf_ref[...]
    cos_ref[...] = jnp.cos(ang)
    sin_ref[...] = jnp.sin(ang) * sign_ref[...]


def _rope_tables(positions):
    t = positions.size
    half = SW_HEAD_DIM // 2
    inv_freq = ROPE_THETA ** (-jnp.arange(0, SW_HEAD_DIM, 2, dtype=F32) / SW_HEAD_DIM)
    reps = LANES // half
    invf = jnp.tile(inv_freq, reps).reshape(1, LANES)
    sign = jnp.tile(jnp.concatenate([-jnp.ones((half,), F32), jnp.ones((half,), F32)]),
                    LANES // SW_HEAD_DIM).reshape(1, LANES)
    tm = min(TM_PROJ, t)
    row = pl.BlockSpec((1, LANES), lambda i: (0, 0))
    tab = pl.BlockSpec((tm, LANES), lambda i: (i, 0))
    return pl.pallas_call(
        _rope_table_kernel,
        grid=(t // tm,),
        in_specs=[pl.BlockSpec((tm, 1), lambda i: (i, 0)), row, row],
        out_specs=[tab, tab],
        out_shape=[jax.ShapeDtypeStruct((t, LANES), F32)] * 2,
        compiler_params=_cparams(("parallel",)),
        name="rope_tables",
    )(positions.reshape(t, 1), invf, sign)


def _rope(x, cos, sin_signed):
    width = x.shape[-1]
    half = SW_HEAD_DIM // 2
    lane = lax.broadcasted_iota(jnp.int32, x.shape, 1)
    first = (lane % SW_HEAD_DIM) < half
    partner = jnp.where(first, pltpu.roll(x, width - half, 1), pltpu.roll(x, half, 1))
    return x * cos + partner * sin_signed


def _inproj_kernel(x_ref, w_ref, wabt_ref, cos_ref, sin_ref,
                   qkv_ref, z_ref, qsw_ref, kvsw_ref, ab_ref, abt_ref):
    xb = x_ref[...].astype(BF16)

    def mm(cols):
        return jnp.dot(xb, w_ref[0, :, cols[0]:cols[1]], preferred_element_type=F32)

    qkv_ref[...] = mm(C_QKV).astype(BF16)
    z_ref[...] = mm(C_Z).astype(BF16)
    cos = cos_ref[...]
    sin = sin_ref[...]
    reps = SW_WIDTH // LANES
    q = _rope(mm(C_QSW), jnp.tile(cos, (1, reps)), jnp.tile(sin, (1, reps))) * (SW_HEAD_DIM ** -0.5)
    qsw_ref[...] = q.astype(BF16)
    kv = mm(C_KVSW)
    k = _rope(kv[:, :SW_KV_WIDTH], cos, sin)
    kvsw_ref[...] = jnp.concatenate([k, kv[:, SW_KV_WIDTH:]], axis=1).astype(BF16)
    ab_ref[...] = mm(C_AB)
    abt_ref[...] = lax.dot_general(wabt_ref[0], xb, (((1,), (1,)), ((), ())),
                                   preferred_element_type=F32)


def _inproj(x2d, w_packed, wabt, layer, cos, sin):
    t = x2d.shape[0]
    tm = min(TM_PROJ, t)

    def rows(width):
        return pl.BlockSpec((tm, width), lambda i: (i, 0))

    return pl.pallas_call(
        _inproj_kernel,
        grid=(t // tm,),
        in_specs=[rows(D_MODEL),
                  pl.BlockSpec((1, D_MODEL, IN_COLS_PACKED), lambda i: (layer, 0, 0)),
                  pl.BlockSpec((1, SUBLANES, D_MODEL), lambda i: (layer, 0, 0)),
                  rows(LANES), rows(LANES)],
        out_specs=[rows(3 * DN_WIDTH), rows(DN_WIDTH), rows(SW_WIDTH), rows(2 * SW_KV_WIDTH),
                   rows(LANES), pl.BlockSpec((SUBLANES, tm), lambda i: (0, i))],
        out_shape=[jax.ShapeDtypeStruct((t, 3 * DN_WIDTH), BF16),
                   jax.ShapeDtypeStruct((t, DN_WIDTH), BF16),
                   jax.ShapeDtypeStruct((t, SW_WIDTH), BF16),
                   jax.ShapeDtypeStruct((t, 2 * SW_KV_WIDTH), BF16),
                   jax.ShapeDtypeStruct((t, LANES), F32),
                   jax.ShapeDtypeStruct((SUBLANES, t), F32)],
        compiler_params=_cparams(("parallel",)),
        name="in_proj",
    )(x2d, w_packed, wabt, cos, sin)


def _bmm(a, b):
    return lax.dot_general(a, b, (((2,), (1,)), ((0,), (0,))), preferred_element_type=F32)


def _bmm_nt(a, b):
    return lax.dot_general(a, b, (((2,), (2,)), ((0,), (0,))), preferred_element_type=F32)


def _unit_lower_inverse(lmat, row, col):
    same16 = ((row // 16) == (col // 16))[None]
    same32 = ((row // 32) == (col // 32))[None]
    eye = (row == col).astype(F32)[None]
    ld = jnp.where(same16, lmat, 0.0)
    inv = eye - ld
    power = ld.astype(BF16)
    for _ in range(3):
        power = _bmm(power, power).astype(BF16)
        inv = inv + _bmm(inv.astype(BF16), power)
    lmat_bf16 = lmat.astype(BF16)
    for off_diag in (same32 & ~same16, ~same32):
        c = jnp.where(off_diag, lmat_bf16, jnp.zeros_like(lmat_bf16))
        inv_bf16 = inv.astype(BF16)
        inv = inv - _bmm(_bmm(inv_bf16, c).astype(BF16), inv_bf16)
    return inv


def _dn_prep_kernel(qkv_ref, halo_ref, ab_ref, abt_ref, convw_ref, prow_ref, pcol_ref,
                    u_ref, w_ref, qd_ref, attn_ref, ket_ref, gl_ref, xp_ref, *, blocks_per_seq):
    tb = qkv_ref.shape[0]
    first = (pl.program_id(0) % blocks_per_seq) == 0

    xp_ref[0:DN_HALO, :] = jnp.where(first, 0.0, halo_ref[...].astype(F32))
    xp_ref[DN_HALO:DN_HALO + tb, :] = qkv_ref[...].astype(F32)
    convw = convw_ref[...]
    y = None
    for tap in range(CONV_WIDTH):
        start = DN_HALO - (CONV_WIDTH - 1) + tap
        term = xp_ref[start:start + tb, :] * convw[tap:tap + 1, :]
        y = term if y is None else y + term
    qkv = _silu(y)

    row = lax.broadcasted_iota(jnp.int32, (DN_SUB, DN_SUB), 0)
    col = lax.broadcasted_iota(jnp.int32, (DN_SUB, DN_SUB), 1)
    same_chunk = (row // CHUNK) == (col // CHUNK)
    lower_incl = same_chunk & (row >= col)
    strict_lower = same_chunk & (row > col)
    cum_lhs = jnp.concatenate([lower_incl.astype(F32), same_chunk.astype(F32)], axis=0)
    cum_rhs_t = (same_chunk & (row <= col)).astype(F32)

    ab = ab_ref[...]
    g_cols = -jnp.exp(prow_ref[0:1, :]) * _softplus(ab + prow_ref[1:2, :])
    g_rows = -jnp.exp(pcol_ref[:, 0:1]) * _softplus(abt_ref[...] + pcol_ref[:, 1:2])
    beta_cols = jax.nn.sigmoid(ab)

    nsub = tb // DN_SUB
    subs = [slice(s * DN_SUB, (s + 1) * DN_SUB) for s in range(nsub)]
    gl_rows, gcum_cols, gtot_cols, gcum_rows = [], [], [], []
    for rs in subs:
        cum = jnp.dot(cum_lhs, g_cols[rs, :], preferred_element_type=F32,
                      precision=lax.Precision.HIGHEST)
        gcum_cols.append(cum[:DN_SUB])
        gtot_cols.append(cum[DN_SUB:])
        gcum_rows.append(jnp.dot(g_rows[:, rs], cum_rhs_t, preferred_element_type=F32,
                                 precision=lax.Precision.HIGHEST))
        for c in range(DN_SUB // CHUNK):
            tot = jnp.exp(cum[DN_SUB + c * CHUNK:DN_SUB + c * CHUNK + 1, :])
            gl_rows.append(jnp.concatenate(
                [jnp.broadcast_to(tot[:, h:h + 1], (1, DN_HEAD_DIM)) for h in range(DN_HEADS)], axis=1))
    gl_ref[...] = jnp.concatenate(gl_rows, axis=0)

    chains = [(s, h) for s in range(nsub) for h in range(DN_HEADS)]

    def stack(fn):
        return jnp.stack([fn(s, h) for s, h in chains], axis=0)

    def head_cols(base):
        return stack(lambda s, h: qkv[subs[s], base + h * DN_HEAD_DIM:base + (h + 1) * DN_HEAD_DIM])

    q = head_cols(0)
    k = head_cols(DN_WIDTH)
    v = head_cols(2 * DN_WIDTH)
    gc = stack(lambda s, h: gcum_cols[s][:, h:h + 1])
    gt = stack(lambda s, h: gtot_cols[s][:, h:h + 1])
    gr = stack(lambda s, h: gcum_rows[s][h:h + 1, :])
    beta = stack(lambda s, h: beta_cols[subs[s], DN_HEADS + h:DN_HEADS + h + 1])

    q = q * lax.rsqrt(jnp.sum(q * q, axis=-1, keepdims=True) + 1e-6) * (DN_HEAD_DIM ** -0.5)
    k = k * lax.rsqrt(jnp.sum(k * k, axis=-1, keepdims=True) + 1e-6)
    lower3 = lower_incl[None]
    decay = jnp.where(lower3, jnp.exp(jnp.where(lower3, gc - gr, 0.0)), 0.0)
    k_beta = k * beta
    gram = _bmm_nt(jnp.concatenate([q, k_beta], axis=1).astype(BF16), k.astype(BF16))
    attn = jnp.where(lower3, gram[:, :DN_SUB] * decay, 0.0)
    lmat = jnp.where(strict_lower[None], gram[:, DN_SUB:] * decay, 0.0)
    tinv = _unit_lower_inverse(lmat, row, col)
    exp_gc = jnp.exp(gc)
    uw = _bmm(tinv.astype(BF16), jnp.concatenate([v * beta, k_beta * exp_gc], axis=2).astype(BF16))
    q_dec = (q * exp_gc).astype(BF16)
    k_end = k * jnp.exp(gt - gc)
    compact = attn[:, :, 0:CHUNK]
    for c in range(1, DN_SUB // CHUNK):
        compact = compact + attn[:, :, c * CHUNK:(c + 1) * CHUNK]
    compact = compact.astype(BF16)
    uw = uw.astype(BF16)

    for idx, (s, h) in enumerate(chains):
        rs = subs[s]
        hs = slice(h * DN_HEAD_DIM, (h + 1) * DN_HEAD_DIM)
        u_ref[rs, hs] = uw[idx, :, :DN_HEAD_DIM]
        w_ref[rs, hs] = uw[idx, :, DN_HEAD_DIM:]
        qd_ref[rs, hs] = q_dec[idx]
        ket_ref[0, hs, rs] = k_end[idx].T.astype(BF16)
        attn_ref[rs, h * CHUNK:(h + 1) * CHUNK] = compact[idx]


def _dn_scan_kernel(u_ref, w_ref, qd_ref, attn_ref, ket_ref, z_ref, gl_ref, normw_ref,
                    o_ref, state_ref):
    step = pl.program_id(0)

    @pl.when(step == 0)
    def _():
        state_ref[...] = jnp.zeros_like(state_ref)

    batch = u_ref.shape[0]
    chunks = DN_SCAN // CHUNK
    normw = normw_ref[...]
    for b in range(batch):
        for h in range(DN_HEADS):
            hs = slice(h * DN_HEAD_DIM, (h + 1) * DN_HEAD_DIM)
            state = state_ref[b * DN_HEADS + h]
            for c in range(chunks):
                rs = slice(c * CHUNK, (c + 1) * CHUNK)
                wq = jnp.concatenate([w_ref[b, rs, hs], qd_ref[b, rs, hs]], axis=0)
                ws_qs = jnp.dot(wq, state.astype(BF16), preferred_element_type=F32)
                v_new = u_ref[b, rs, hs].astype(F32) - ws_qs[:CHUNK]
                ak = jnp.concatenate([attn_ref[b, rs, h * CHUNK:(h + 1) * CHUNK], ket_ref[b, hs, rs]],
                                     axis=0)
                av_kv = jnp.dot(ak, v_new.astype(BF16), preferred_element_type=F32)
                o = ws_qs[CHUNK:] + av_kv[:CHUNK]
                g_last = gl_ref[b, 0, c:c + 1, hs]
                state = state * g_last + av_kv[CHUNK:]
                o = o * lax.rsqrt(jnp.mean(o * o, axis=-1, keepdims=True) + RMS_EPS)
                o_ref[b, rs, hs] = (o * normw * _silu(z_ref[b, rs, hs].astype(F32))).astype(o_ref.dtype)
            state_ref[b * DN_HEADS + h] = state


def _deltanet(qkv, z, ab, abt, conv_w, a_log, dt_bias, dn_norm_w, batch, seq):
    t = batch * seq
    tb = min(DN_BLOCK, seq)
    blocks_per_seq = seq // tb
    prow = jnp.zeros((SUBLANES, LANES), F32).at[0, :DN_HEADS].set(a_log).at[1, :DN_HEADS].set(dt_bias)
    pcol = (jnp.zeros((SUBLANES, LANES), F32)
            .at[:DN_HEADS, 0].set(a_log).at[DN_HEADS:2 * DN_HEADS, 0].set(a_log)
            .at[:DN_HEADS, 1].set(dt_bias).at[DN_HEADS:2 * DN_HEADS, 1].set(dt_bias))

    def rows(width):
        return pl.BlockSpec((tb, width), lambda i: (i, 0))

    def const(shape):
        return pl.BlockSpec(shape, lambda i: (0,) * len(shape))

    u, w, qd, attn, ket, gl = pl.pallas_call(
        functools.partial(_dn_prep_kernel, blocks_per_seq=blocks_per_seq),
        grid=(t // tb,),
        in_specs=[rows(3 * DN_WIDTH),
                  pl.BlockSpec((DN_HALO, 3 * DN_WIDTH),
                               lambda i: (jnp.maximum(i * (tb // DN_HALO) - 1, 0), 0)),
                  rows(LANES), pl.BlockSpec((SUBLANES, tb), lambda i: (0, i)),
                  const((CONV_WIDTH, 3 * DN_WIDTH)), const((SUBLANES, LANES)), const((SUBLANES, LANES))],
        out_specs=[rows(DN_WIDTH), rows(DN_WIDTH), rows(DN_WIDTH), rows(DN_HEADS * CHUNK),
                   pl.BlockSpec((1, DN_WIDTH, tb), lambda i: (i // blocks_per_seq, 0, i % blocks_per_seq)),
                   pl.BlockSpec((tb // CHUNK, DN_WIDTH), lambda i: (i, 0))],
        out_shape=[jax.ShapeDtypeStruct((t, DN_WIDTH), BF16)] * 3
        + [jax.ShapeDtypeStruct((t, DN_HEADS * CHUNK), BF16),
           jax.ShapeDtypeStruct((batch, DN_WIDTH, seq), BF16),
           jax.ShapeDtypeStruct((t // CHUNK, DN_WIDTH), F32)],
        scratch_shapes=[pltpu.VMEM((DN_HALO + tb, 3 * DN_WIDTH), F32)],
        compiler_params=_cparams(("parallel",)),
        name="deltanet_prep",
    )(qkv, qkv, ab, abt, conv_w, prow, pcol)

    def seq_rows(width):
        return pl.BlockSpec((batch, DN_SCAN, width), lambda c: (0, c, 0))

    def view(a):
        return a.reshape(batch, seq, a.shape[-1])

    o = pl.pallas_call(
        _dn_scan_kernel,
        grid=(seq // DN_SCAN,),
        in_specs=[seq_rows(DN_WIDTH), seq_rows(DN_WIDTH), seq_rows(DN_WIDTH), seq_rows(DN_HEADS * CHUNK),
                  pl.BlockSpec((batch, DN_WIDTH, DN_SCAN), lambda c: (0, 0, c)),
                  seq_rows(DN_WIDTH),
                  pl.BlockSpec((batch, 1, DN_SCAN // CHUNK, DN_WIDTH), lambda c: (0, c, 0, 0)),
                  pl.BlockSpec((1, DN_HEAD_DIM), lambda c: (0, 0))],
        out_specs=seq_rows(DN_WIDTH),
        out_shape=jax.ShapeDtypeStruct((batch, seq, DN_WIDTH), BF16),
        scratch_shapes=[pltpu.VMEM((batch * DN_HEADS, DN_HEAD_DIM, DN_HEAD_DIM), F32)],
        compiler_params=_cparams(("arbitrary",)),
        name="deltanet_scan",
    )(view(u), view(w), view(qd), view(attn), ket, view(z),
      gl.reshape(batch, seq // DN_SCAN, DN_SCAN // CHUNK, DN_WIDTH), dn_norm_w.reshape(1, DN_HEAD_DIM))
    return o.reshape(t, DN_WIDTH)


def _swa_kernel(q_ref, kvc_ref, kvp_ref, sink_ref, o_ref):
    qi = lax.broadcasted_iota(jnp.int32, (WINDOW, 2 * WINDOW), 0)
    kj = lax.broadcasted_iota(jnp.int32, (WINDOW, 2 * WINDOW), 1)
    in_window = (kj > qi) & (kj <= qi + WINDOW)
    first_valid = in_window & ((kj >= WINDOW) | (pl.program_id(1) > 0))
    kv = jnp.concatenate([kvp_ref[...], kvc_ref[...]], axis=0)
    scores = []
    for j in range(SWA_BLOCKS):
        band = slice(j * WINDOW, (j + 2) * WINDOW)
        valid = first_valid if j == 0 else in_window
        for hq in range(SW_Q_HEADS):
            hk = hq // SW_GROUP
            q = q_ref[j * WINDOW:(j + 1) * WINDOW, hq * SW_HEAD_DIM:(hq + 1) * SW_HEAD_DIM]
            s = _bdot_nt(q, kv[band, hk * SW_HEAD_DIM:(hk + 1) * SW_HEAD_DIM])
            scores.append(jnp.where(valid, s, -jnp.inf))
    s = jnp.concatenate(scores, axis=0)
    sink = sink_ref[...]
    m = jnp.maximum(jnp.max(s, axis=-1, keepdims=True), sink)
    p = jnp.exp(s - m)
    inv_denom = 1.0 / (jnp.sum(p, axis=-1, keepdims=True) + jnp.exp(sink - m))
    p = p.astype(BF16)
    for j in range(SWA_BLOCKS):
        band = slice(j * WINDOW, (j + 2) * WINDOW)
        outs = []
        for hq in range(SW_Q_HEADS):
            hk = hq // SW_GROUP
            rs = slice((j * SW_Q_HEADS + hq) * WINDOW, (j * SW_Q_HEADS + hq + 1) * WINDOW)
            v = kv[band, SW_KV_WIDTH + hk * SW_HEAD_DIM:SW_KV_WIDTH + (hk + 1) * SW_HEAD_DIM]
            outs.append(jnp.dot(p[rs], v, preferred_element_type=F32) * inv_denom[rs])
        o_ref[j * WINDOW:(j + 1) * WINDOW, :] = jnp.concatenate(outs, axis=1).astype(o_ref.dtype)


def _swa(qsw, kvsw, sinks, batch, seq):
    t = batch * seq
    tq = SWA_BLOCKS * WINDOW
    nstep = seq // tq
    sink_rows = jnp.tile(jnp.repeat(sinks.astype(F32), WINDOW), SWA_BLOCKS).reshape(-1, 1)
    return pl.pallas_call(
        _swa_kernel,
        grid=(batch, nstep),
        in_specs=[pl.BlockSpec((tq, SW_WIDTH), lambda b, i: (b * nstep + i, 0)),
                  pl.BlockSpec((tq, 2 * SW_KV_WIDTH), lambda b, i: (b * nstep + i, 0)),
                  pl.BlockSpec((WINDOW, 2 * SW_KV_WIDTH),
                               lambda b, i: ((b * nstep + i) * SWA_BLOCKS - jnp.minimum(i, 1), 0)),
                  pl.BlockSpec((SWA_BLOCKS * SW_Q_HEADS * WINDOW, 1), lambda b, i: (0, 0))],
        out_specs=pl.BlockSpec((tq, SW_WIDTH), lambda b, i: (b * nstep + i, 0)),
        out_shape=jax.ShapeDtypeStruct((t, SW_WIDTH), BF16),
        compiler_params=_cparams(("parallel", "parallel")),
        name="sliding_window_attn",
    )(qsw, kvsw, kvsw, sink_rows)


def _outproj_kernel(odn_ref, osw_ref, w_ref, x_ref, g_ref, b_ref, o_ref):
    mix = (jnp.dot(odn_ref[...], w_ref[0, 0:DN_WIDTH, :], preferred_element_type=F32)
           + jnp.dot(osw_ref[...], w_ref[0, DN_WIDTH:, :], preferred_element_type=F32))
    o_ref[...] = _layer_norm(ALPHA * x_ref[...] + mix, g_ref[...], b_ref[...])


def _outproj_ln(o_dn, o_sw, w_out_bf16, layer, x2d, g, b):
    t = x2d.shape[0]
    tm = min(TM_PROJ, t)

    def rows(width):
        return pl.BlockSpec((tm, width), lambda i: (i, 0))

    vec = pl.BlockSpec((1, D_MODEL), lambda i: (0, 0))
    return pl.pallas_call(
        _outproj_kernel,
        grid=(t // tm,),
        in_specs=[rows(DN_WIDTH), rows(SW_WIDTH),
                  pl.BlockSpec((1, DN_WIDTH + SW_WIDTH, D_MODEL), lambda i: (layer, 0, 0)),
                  rows(D_MODEL), vec, vec],
        out_specs=rows(D_MODEL),
        out_shape=jax.ShapeDtypeStruct((t, D_MODEL), F32),
        compiler_params=_cparams(("parallel",)),
        name="out_proj_ln",
    )(o_dn, o_sw, w_out_bf16, x2d, g.reshape(1, D_MODEL), b.reshape(1, D_MODEL))


def _ffn_kernel(x_ref, wg_ref, wu_ref, wd_ref, g_ref, b_ref, o_ref, xb_ref, acc_ref):
    f = pl.program_id(1)

    @pl.when(f == 0)
    def _():
        xb_ref[...] = x_ref[...].astype(BF16)
        acc_ref[...] = jnp.zeros_like(acc_ref)

    xb = xb_ref[...]
    gate = jnp.dot(xb, wg_ref[0].astype(BF16), preferred_element_type=F32)
    up = jnp.dot(xb, wu_ref[0].astype(BF16), preferred_element_type=F32)
    hid = (_silu(gate) * up).astype(BF16)
    acc_ref[...] += jnp.dot(hid, wd_ref[0].astype(BF16), preferred_element_type=F32)

    @pl.when(f == pl.num_programs(1) - 1)
    def _():
        o_ref[...] = _layer_norm(ALPHA * x_ref[...] + acc_ref[...], g_ref[...], b_ref[...])


def _ffn_ln(x2d, w_gate, w_up, w_down, idx, g, b):
    t = x2d.shape[0]
    tm = min(TM_FFN, t)
    vec = pl.BlockSpec((1, D_MODEL), lambda i, f: (0, 0))
    return pl.pallas_call(
        _ffn_kernel,
        grid=(t // tm, D_FF // TF_FFN),
        in_specs=[pl.BlockSpec((tm, D_MODEL), lambda i, f: (i, 0)),
                  pl.BlockSpec((1, D_MODEL, TF_FFN), lambda i, f: (idx, 0, f)),
                  pl.BlockSpec((1, D_MODEL, TF_FFN), lambda i, f: (idx, 0, f)),
                  pl.BlockSpec((1, TF_FFN, D_MODEL), lambda i, f: (idx, f, 0)),
                  vec, vec],
        out_specs=pl.BlockSpec((tm, D_MODEL), lambda i, f: (i, 0)),
        out_shape=jax.ShapeDtypeStruct((t, D_MODEL), F32),
        scratch_shapes=[pltpu.VMEM((tm, D_MODEL), BF16), pltpu.VMEM((tm, D_MODEL), F32)],
        compiler_params=_cparams(("parallel", "arbitrary")),
        name="dense_swiglu_ln",
    )(x2d, w_gate, w_up, w_down, g.reshape(1, D_MODEL), b.reshape(1, D_MODEL))


R_E1, R_E2, R_RANK1, R_RANK2, R_G1, R_G2 = range(6)


def _router_kernel(x_ref, wr_ref, meta_ref, count_ref, carry_ref):
    i = pl.program_id(0)

    @pl.when(i == 0)
    def _():
        carry_ref[...] = jnp.zeros_like(carry_ref)

    tm = x_ref.shape[0]
    logits = jnp.dot(x_ref[...], wr_ref[...], preferred_element_type=F32,
                     precision=lax.Precision.HIGHEST)
    lane = lax.broadcasted_iota(jnp.int32, logits.shape, 1)
    logits = jnp.where(lane < N_EXPERTS, logits, -jnp.inf)
    m1 = jnp.max(logits, axis=-1, keepdims=True)
    e1 = jnp.min(jnp.where(logits == m1, lane, LANES), axis=-1, keepdims=True)
    rest = jnp.where(lane == e1, -jnp.inf, logits)
    m2 = jnp.max(rest, axis=-1, keepdims=True)
    e2 = jnp.min(jnp.where(rest == m2, lane, LANES), axis=-1, keepdims=True)
    ex = jnp.exp(m2 - m1)
    g1 = 1.0 / (1.0 + ex)
    g2 = ex / (1.0 + ex)

    hit = ((lane == e1) | (lane == e2)).astype(BF16)
    r = lax.broadcasted_iota(jnp.int32, (tm, tm), 0)
    c = lax.broadcasted_iota(jnp.int32, (tm, tm), 1)
    before = jnp.dot((c < r).astype(BF16), hit, preferred_element_type=F32) + carry_ref[0:1, :]
    rank1 = jnp.sum(jnp.where(lane == e1, before, 0.0), axis=-1, keepdims=True)
    rank2 = jnp.sum(jnp.where(lane == e2, before, 0.0), axis=-1, keepdims=True)
    carry_ref[0:1, :] = carry_ref[0:1, :] + jnp.sum(hit.astype(F32), axis=0, keepdims=True)
    count_ref[...] = jnp.broadcast_to(carry_ref[0:1, :], count_ref.shape)

    meta = jnp.zeros((tm, SUBLANES), F32)
    lane8 = lax.broadcasted_iota(jnp.int32, (tm, SUBLANES), 1)
    for idx, val in ((R_E1, e1.astype(F32)), (R_E2, e2.astype(F32)), (R_RANK1, rank1),
                     (R_RANK2, rank2), (R_G1, g1), (R_G2, g2)):
        meta = jnp.where(lane8 == idx, val, meta)
    meta_ref[...] = meta


def _router(x2d, router_w):
    t = x2d.shape[0]
    tm = min(TM_ROUTE, t)
    wr = jnp.zeros((D_MODEL, LANES), F32).at[:, :N_EXPERTS].set(router_w)
    return pl.pallas_call(
        _router_kernel,
        grid=(t // tm,),
        in_specs=[pl.BlockSpec((tm, D_MODEL), lambda i: (i, 0)),
                  pl.BlockSpec((D_MODEL, LANES), lambda i: (0, 0))],
        out_specs=[pl.BlockSpec((tm, SUBLANES), lambda i: (i, 0)),
                   pl.BlockSpec((SUBLANES, LANES), lambda i: (0, 0))],
        out_shape=[jax.ShapeDtypeStruct((t, SUBLANES), F32),
                   jax.ShapeDtypeStruct((SUBLANES, LANES), F32)],
        scratch_shapes=[pltpu.VMEM((SUBLANES, LANES), F32)],
        compiler_params=_cparams(("arbitrary",)),
        name="moe_router",
    )(x2d, wr)


def _row_copy(src_ref, src_row, dst_ref, dst_row, sem):
    return pltpu.make_async_copy(src_ref.at[pl.ds(src_row, 1), :], dst_ref.at[pl.ds(dst_row, 1), :], sem)


def _dispatch_kernel(dest1_ref, dest2_ref, x_ref, xs_ref, sem):
    base = pl.program_id(0) * TM_MOVE

    def issue(j, carry):
        tok = base + j
        _row_copy(x_ref, j, xs_ref, dest1_ref[tok], sem).start(priority=0)
        _row_copy(x_ref, j, xs_ref, dest2_ref[tok], sem).start(priority=1)
        return carry

    lax.fori_loop(0, TM_MOVE, issue, 0, unroll=MOVE_UNROLL)
    for _ in range(2):
        pltpu.make_async_copy(x_ref, xs_ref.at[pl.ds(0, TM_MOVE), :], sem).wait()


def _dispatch(x2d, dest1, dest2, rows_sorted):
    t = x2d.shape[0]
    return pl.pallas_call(
        _dispatch_kernel,
        grid_spec=pltpu.PrefetchScalarGridSpec(
            num_scalar_prefetch=2,
            grid=(t // TM_MOVE,),
            in_specs=[pl.BlockSpec((TM_MOVE, D_MODEL), lambda i, d1, d2: (i, 0))],
            out_specs=pl.BlockSpec(memory_space=pl.ANY),
            scratch_shapes=[pltpu.SemaphoreType.DMA(())]),
        out_shape=jax.ShapeDtypeStruct((rows_sorted, D_MODEL), F32),
        compiler_params=_cparams(("arbitrary",)),
        name="moe_dispatch",
    )(dest1, dest2, x2d)


def _expert_kernel(vtile_ref, vexpert_ref, vfidx_ref, vlo_ref, vhi_ref, vfirst_ref, vlast_ref,
                   xs_ref, wg_ref, wu_ref, wd_ref, ys_ref, xb_ref, acc_ref):
    v = pl.program_id(0)
    f = pl.program_id(1)
    lo = vlo_ref[v]
    hi = vhi_ref[v]

    @pl.when(hi > lo)
    def _():
        @pl.when(f == 0)
        def _():
            rowid = lax.broadcasted_iota(jnp.int32, xs_ref.shape, 0)
            xb_ref[...] = jnp.where((rowid >= lo) & (rowid < hi), xs_ref[...], 0.0).astype(BF16)

        @pl.when((f == 0) & (vfirst_ref[v] == 1))
        def _():
            acc_ref[...] = jnp.zeros_like(acc_ref)

        xb = xb_ref[...]
        gate = jnp.dot(xb, wg_ref[0, 0].astype(BF16), preferred_element_type=F32)
        up = jnp.dot(xb, wu_ref[0, 0].astype(BF16), preferred_element_type=F32)
        hid = (_silu(gate) * up).astype(BF16)
        acc_ref[...] += jnp.dot(hid, wd_ref[0, 0].astype(BF16), preferred_element_type=F32)

        @pl.when((f == pl.num_programs(1) - 1) & (vlast_ref[v] == 1))
        def _():
            ys_ref[...] = acc_ref[...]


def _expert_visits(counts, n_tiles):
    n_visits = n_tiles + N_EXPERTS - 1
    ends = jnp.cumsum(counts)
    offs = ends - counts
    first_tile = offs // TM_EXPERT
    per_expert = jnp.where(counts > 0, (ends - 1) // TM_EXPERT - first_tile + 1, 0)
    vend = jnp.cumsum(per_expert)
    vstart = vend - per_expert
    total = vend[-1]
    vid = jnp.arange(n_visits, dtype=jnp.int32)
    real = vid < total
    vid_c = jnp.minimum(vid, jnp.maximum(total - 1, 0))
    vexpert = jnp.minimum(jnp.sum(vid_c[:, None] >= vend[None, :], axis=1), N_EXPERTS - 1)
    vtile = first_tile[vexpert] + vid_c - vstart[vexpert]
    lo = jnp.maximum(offs[vexpert], vtile * TM_EXPERT) - vtile * TM_EXPERT
    hi = jnp.minimum(ends[vexpert], (vtile + 1) * TM_EXPERT) - vtile * TM_EXPERT
    prev_tile = jnp.concatenate([jnp.full((1,), -1, vtile.dtype), vtile[:-1]])
    next_tile = jnp.concatenate([vtile[1:], jnp.full((1,), -1, vtile.dtype)])
    first = real & (vtile != prev_tile)
    last = real & ((vtile != next_tile) | (vid == total - 1))
    i32 = lambda a: a.astype(jnp.int32)
    fpin = jnp.where(real, -1, D_FF // TF_FFN - 1)
    return (i32(vtile), i32(vexpert), i32(fpin), i32(jnp.where(real, lo, 0)),
            i32(jnp.where(real, hi, 0)), i32(first), i32(last), i32(offs))


def _expert_ffn(xs, visits, w_gate, w_up, w_down, idx):
    rows_sorted = xs.shape[0]
    n_visits = visits[0].shape[0]
    nf = D_FF // TF_FFN

    def f_idx(v, f, vf):
        return jnp.where(vf[v] >= 0, vf[v], f)

    def rows_map(v, f, vt, ve, vf, *_):
        return (vt[v], 0)

    def wcol_map(v, f, vt, ve, vf, *_):
        return (idx, ve[v], 0, f_idx(v, f, vf))

    def wrow_map(v, f, vt, ve, vf, *_):
        return (idx, ve[v], f_idx(v, f, vf), 0)

    return pl.pallas_call(
        _expert_kernel,
        grid_spec=pltpu.PrefetchScalarGridSpec(
            num_scalar_prefetch=len(visits),
            grid=(n_visits, nf),
            in_specs=[pl.BlockSpec((TM_EXPERT, D_MODEL), rows_map),
                      pl.BlockSpec((1, 1, D_MODEL, TF_FFN), wcol_map),
                      pl.BlockSpec((1, 1, D_MODEL, TF_FFN), wcol_map),
                      pl.BlockSpec((1, 1, TF_FFN, D_MODEL), wrow_map)],
            out_specs=pl.BlockSpec((TM_EXPERT, D_MODEL), rows_map),
            scratch_shapes=[pltpu.VMEM((TM_EXPERT, D_MODEL), BF16),
                            pltpu.VMEM((TM_EXPERT, D_MODEL), F32)]),
        out_shape=jax.ShapeDtypeStruct((rows_sorted, D_MODEL), F32),
        compiler_params=_cparams(("arbitrary", "arbitrary")),
        name="moe_expert_swiglu",
    )(*visits, xs, w_gate, w_up, w_down)


def _combine_kernel(dest1_ref, dest2_ref, ys_ref, x_ref, meta_ref, g_ref, b_ref, o_ref,
                    buf1_ref, buf2_ref, sem):
    base = pl.program_id(0) * TM_MOVE

    def issue(j, carry):
        tok = base + j
        _row_copy(ys_ref, dest1_ref[tok], buf1_ref, j, sem).start(priority=0)
        _row_copy(ys_ref, dest2_ref[tok], buf2_ref, j, sem).start(priority=1)
        return carry

    lax.fori_loop(0, TM_MOVE, issue, 0, unroll=MOVE_UNROLL)
    for buf_ref in (buf1_ref, buf2_ref):
        pltpu.make_async_copy(ys_ref.at[pl.ds(0, TM_MOVE), :], buf_ref, sem).wait()

    meta = meta_ref[...]
    g1 = meta[:, R_G1:R_G1 + 1]
    g2 = meta[:, R_G2:R_G2 + 1]
    moe = g1 * buf1_ref[...] + g2 * buf2_ref[...]
    o_ref[...] = _layer_norm(ALPHA * x_ref[...] + moe, g_ref[...], b_ref[...])


def _combine_ln(ys, dest1, dest2, x2d, meta, g, b):
    t = x2d.shape[0]
    vec = pl.BlockSpec((1, D_MODEL), lambda i, d1, d2: (0, 0))
    return pl.pallas_call(
        _combine_kernel,
        grid_spec=pltpu.PrefetchScalarGridSpec(
            num_scalar_prefetch=2,
            grid=(t // TM_MOVE,),
            in_specs=[pl.BlockSpec(memory_space=pl.ANY),
                      pl.BlockSpec((TM_MOVE, D_MODEL), lambda i, d1, d2: (i, 0)),
                      pl.BlockSpec((TM_MOVE, SUBLANES), lambda i, d1, d2: (i, 0)),
                      vec, vec],
            out_specs=pl.BlockSpec((TM_MOVE, D_MODEL), lambda i, d1, d2: (i, 0)),
            scratch_shapes=[pltpu.VMEM((TM_MOVE, D_MODEL), F32),
                            pltpu.VMEM((TM_MOVE, D_MODEL), F32),
                            pltpu.SemaphoreType.DMA(())]),
        out_shape=jax.ShapeDtypeStruct((t, D_MODEL), F32),
        compiler_params=_cparams(("arbitrary",)),
        name="moe_combine_ln",
    )(dest1, dest2, ys, x2d, meta, g.reshape(1, D_MODEL), b.reshape(1, D_MODEL))


def _moe_ln(x2d, router_w, w_gate, w_up, w_down, idx, g, b):
    t = x2d.shape[0]
    meta, counts = _router(x2d, router_w)
    counts = counts[0, :N_EXPERTS].astype(jnp.int32)
    *visits, group_start = _expert_visits(counts, (2 * t) // TM_EXPERT)
    e1 = meta[:, R_E1].astype(jnp.int32)
    e2 = meta[:, R_E2].astype(jnp.int32)
    dest1 = group_start[e1] + meta[:, R_RANK1].astype(jnp.int32)
    dest2 = group_start[e2] + meta[:, R_RANK2].astype(jnp.int32)
    xs = _dispatch(x2d, dest1, dest2, 2 * t)
    ys = _expert_ffn(xs, tuple(visits), w_gate, w_up, w_down, idx)
    return _combine_ln(ys, dest1, dest2, x2d, meta, g, b)


def _pack_w_in(w):
    dn = 4 * DN_WIDTH
    ab = w[..., dn:dn + 2 * DN_HEADS]
    pad = jnp.zeros(w.shape[:-1] + (LANES - 2 * DN_HEADS,), w.dtype)
    packed = jnp.concatenate([w[..., :dn], w[..., dn + 2 * DN_HEADS:], ab, pad], axis=-1).astype(BF16)
    return packed, jnp.swapaxes(ab, -1, -2).astype(BF16)


@jax.jit
def _trunk(x, positions, w_in, conv_w, a_log, dt_bias, dn_norm_w, sinks, w_out, ln_g, ln_b,
           ffn_w_gate, ffn_w_up, ffn_w_down, router_w, moe_w_gate, moe_w_up, moe_w_down):
    batch, seq, _ = x.shape
    x2d = x.reshape(batch * seq, D_MODEL)
    cos, sin = _rope_tables(positions)
    w_packed, wabt = _pack_w_in(w_in)
    w_out_bf16 = w_out.astype(BF16)
    for layer in range(DEPTH):
        qkv, z, qsw, kvsw, ab, abt = _inproj(x2d, w_packed, wabt, layer, cos, sin)
        o_dn = _deltanet(qkv, z, ab, abt, conv_w[layer], a_log[layer], dt_bias[layer],
                         dn_norm_w[layer], batch, seq)
        o_sw = _swa(qsw, kvsw, sinks[layer], batch, seq)
        x2d = _outproj_ln(o_dn, o_sw, w_out_bf16, layer, x2d, ln_g[layer, 0], ln_b[layer, 0])
        i = layer // 2
        if layer % 2 == 0:
            x2d = _ffn_ln(x2d, ffn_w_gate, ffn_w_up, ffn_w_down, i, ln_g[layer, 1], ln_b[layer, 1])
        else:
            x2d = _moe_ln(x2d, router_w[i], moe_w_gate, moe_w_up, moe_w_down, i,
                          ln_g[layer, 1], ln_b[layer, 1])
    return x2d.reshape(batch, seq, D_MODEL)


def kernel(x, positions, w_in, conv_w, a_log, dt_bias, dn_norm_w, sinks, w_out, ln_g, ln_b,
           ffn_w_gate, ffn_w_up, ffn_w_down, router_w, moe_w_gate, moe_w_up, moe_w_down):
    return _trunk(x, positions, w_in, conv_w, a_log, dt_bias, dn_norm_w, sinks, w_out, ln_g, ln_b,
                  ffn_w_gate, ffn_w_up, ffn_w_down, router_w, moe_w_gate, moe_w_up, moe_w_down)
```

```python
import functools

import jax
import jax.numpy as jnp
import numpy as np
from jax import lax
from jax.experimental import pallas as pl
from jax.experimental.pallas import tpu as pltpu

F32 = jnp.float32
BF16 = jnp.bfloat16

D_MODEL = 1024
DEPTH = 4
DN_HEADS = 4
DN_HEAD_DIM = 128
DN_WIDTH = DN_HEADS * DN_HEAD_DIM
CONV_WIDTH = 4
CHUNK = 64
SW_Q_HEADS = 8
SW_KV_HEADS = 2
SW_GROUP = SW_Q_HEADS // SW_KV_HEADS
SW_HEAD_DIM = 64
SW_WIDTH = SW_Q_HEADS * SW_HEAD_DIM
SW_KV_WIDTH = SW_KV_HEADS * SW_HEAD_DIM
WINDOW = 128
ROPE_THETA = 10000.0
D_FF = 3584
N_EXPERTS = 8
ALPHA = (2.0 * DEPTH) ** 0.25
LN_EPS = 1e-5
RMS_EPS = 1e-6

LANES = 128
SUBLANES = 8
DN_HALO = 16
DN_BLOCK = 512
DN_SUB = 128
DN_SCAN = 128
SWA_BLOCKS = 4
VMEM_LIMIT = 56 * 1024 * 1024

C_QKV = (0, 3 * DN_WIDTH)
C_Z = (C_QKV[1], C_QKV[1] + DN_WIDTH)
C_QSW = (C_Z[1], C_Z[1] + SW_WIDTH)
C_KVSW = (C_QSW[1], C_QSW[1] + 2 * SW_KV_WIDTH)
C_AB = (C_KVSW[1], C_KVSW[1] + LANES)
IN_COLS_PACKED = C_AB[1]

TM_PROJ = 512
TM_FFN = 1024
TF_FFN = 512
TM_ROUTE = 1024
TM_EXPERT = 1024
TM_EXPERT_SUB = 256
TM_OUT = 1024
TM_MOVE = 512
MOVE_UNROLL = 8


def _cparams(sem):
    return pltpu.CompilerParams(dimension_semantics=sem, vmem_limit_bytes=VMEM_LIMIT)


def _bdot(a, b):
    return jnp.dot(a.astype(BF16), b.astype(BF16), preferred_element_type=F32)


def _bdot_nt(a, b):
    return lax.dot_general(a.astype(BF16), b.astype(BF16), (((1,), (1,)), ((), ())),
                           preferred_element_type=F32)


def _silu(x):
    return x * jax.nn.sigmoid(x)


def _softplus(x):
    return jnp.maximum(x, 0.0) + jnp.log1p(jnp.exp(-jnp.abs(x)))


def _layer_norm(y, g, b):
    mu = jnp.mean(y, axis=-1, keepdims=True)
    d = y - mu
    var = jnp.mean(d * d, axis=-1, keepdims=True)
    return d * lax.rsqrt(var + LN_EPS) * g + b


def _rope_table_kernel(pos_ref, invf_ref, sign_ref, cos_ref, sin_ref):
    ang = pos_ref[...].astype(F32) * invf_ref[...]
    cos_ref[...] = jnp.cos(ang)
    sin_ref[...] = jnp.sin(ang) * sign_ref[...]


def _rope_tables(positions):
    t = positions.size
    half = SW_HEAD_DIM // 2
    inv_freq = ROPE_THETA ** (-jnp.arange(0, SW_HEAD_DIM, 2, dtype=F32) / SW_HEAD_DIM)
    reps = LANES // half
    invf = jnp.tile(inv_freq, reps).reshape(1, LANES)
    sign = jnp.tile(jnp.concatenate([-jnp.ones((half,), F32), jnp.ones((half,), F32)]),
                    LANES // SW_HEAD_DIM).reshape(1, LANES)
    tm = min(TM_PROJ, t)
    row = pl.BlockSpec((1, LANES), lambda i: (0, 0))
    tab = pl.BlockSpec((tm, LANES), lambda i: (i, 0))
    return pl.pallas_call(
        _rope_table_kernel,
        grid=(t // tm,),
        in_specs=[pl.BlockSpec((tm, 1), lambda i: (i, 0)), row, row],
        out_specs=[tab, tab],
        out_shape=[jax.ShapeDtypeStruct((t, LANES), F32)] * 2,
        compiler_params=_cparams(("parallel",)),
        name="rope_tables",
    )(positions.reshape(t, 1), invf, sign)


def _rope(x, cos, sin_signed):
    width = x.shape[-1]
    half = SW_HEAD_DIM // 2
    lane = lax.broadcasted_iota(jnp.int32, x.shape, 1)
    first = (lane % SW_HEAD_DIM) < half
    partner = jnp.where(first, pltpu.roll(x, width - half, 1), pltpu.roll(x, half, 1))
    return x * cos + partner * sin_signed


def _inproj_kernel(x_ref, w_ref, wabt_ref, cos_ref, sin_ref,
                   qkv_ref, z_ref, qsw_ref, kvsw_ref, ab_ref, abt_ref):
    xb = x_ref[...].astype(BF16)

    def mm(cols):
        return jnp.dot(xb, w_ref[0, :, cols[0]:cols[1]], preferred_element_type=F32)

    qkv_ref[...] = mm(C_QKV).astype(BF16)
    z_ref[...] = mm(C_Z).astype(BF16)
    cos = cos_ref[...]
    sin = sin_ref[...]
    reps = SW_WIDTH // LANES
    q = _rope(mm(C_QSW), jnp.tile(cos, (1, reps)), jnp.tile(sin, (1, reps))) * (SW_HEAD_DIM ** -0.5)
    qsw_ref[...] = q.astype(BF16)
    kv = mm(C_KVSW)
    k = _rope(kv[:, :SW_KV_WIDTH], cos, sin)
    kvsw_ref[...] = jnp.concatenate([k, kv[:, SW_KV_WIDTH:]], axis=1).astype(BF16)
    ab_ref[...] = mm(C_AB)
    abt_ref[...] = lax.dot_general(wabt_ref[0], xb, (((1,), (1,)), ((), ())),
                                   preferred_element_type=F32)


def _inproj(x2d, w_packed, wabt, layer, cos, sin):
    t = x2d.shape[0]
    tm = min(TM_PROJ, t)

    def rows(width):
        return pl.BlockSpec((tm, width), lambda i: (i, 0))

    return pl.pallas_call(
        _inproj_kernel,
        grid=(t // tm,),
        in_specs=[rows(D_MODEL),
                  pl.BlockSpec((1, D_MODEL, IN_COLS_PACKED), lambda i: (layer, 0, 0)),
                  pl.BlockSpec((1, SUBLANES, D_MODEL), lambda i: (layer, 0, 0)),
                  rows(LANES), rows(LANES)],
        out_specs=[rows(3 * DN_WIDTH), rows(DN_WIDTH), rows(SW_WIDTH), rows(2 * SW_KV_WIDTH),
                   rows(LANES), pl.BlockSpec((SUBLANES, tm), lambda i: (0, i))],
        out_shape=[jax.ShapeDtypeStruct((t, 3 * DN_WIDTH), BF16),
                   jax.ShapeDtypeStruct((t, DN_WIDTH), BF16),
                   jax.ShapeDtypeStruct((t, SW_WIDTH), BF16),
                   jax.ShapeDtypeStruct((t, 2 * SW_KV_WIDTH), BF16),
                   jax.ShapeDtypeStruct((t, LANES), F32),
                   jax.ShapeDtypeStruct((SUBLANES, t), F32)],
        compiler_params=_cparams(("parallel",)),
        name="in_proj",
    )(x2d, w_packed, wabt, cos, sin)


def _bmm(a, b):
    return lax.dot_general(a, b, (((2,), (1,)), ((0,), (0,))), preferred_element_type=F32)


def _bmm_nt(a, b):
    return lax.dot_general(a, b, (((2,), (2,)), ((0,), (0,))), preferred_element_type=F32)


def _unit_lower_inverse(lmat, row, col):
    same16 = ((row // 16) == (col // 16))[None]
    same32 = ((row // 32) == (col // 32))[None]
    eye = (row == col).astype(F32)[None]
    ld = jnp.where(same16, lmat, 0.0)
    inv = eye - ld
    power = ld.astype(BF16)
    for _ in range(3):
        power = _bmm(power, power).astype(BF16)
        inv = inv + _bmm(inv.astype(BF16), power)
    lmat_bf16 = lmat.astype(BF16)
    for off_diag in (same32 & ~same16, ~same32):
        c = jnp.where(off_diag, lmat_bf16, jnp.zeros_like(lmat_bf16))
        inv_bf16 = inv.astype(BF16)
        inv = inv - _bmm(_bmm(inv_bf16, c).astype(BF16), inv_bf16)
    return inv


def _dn_prep_kernel(qkv_ref, halo_ref, ab_ref, abt_ref, convw_ref, prow_ref, pcol_ref,
                    u_ref, w_ref, qd_ref, attn_ref, ket_ref, gl_ref, xp_ref, *, blocks_per_seq):
    tb = qkv_ref.shape[0]
    first = (pl.program_id(0) % blocks_per_seq) == 0

    halo = halo_ref[...]
    xp_ref[0:DN_HALO, :] = jnp.where(first, jnp.zeros_like(halo), halo)
    xp_ref[DN_HALO:DN_HALO + tb, :] = qkv_ref[...]
    convw = convw_ref[...]
    span = DN_HALO + DN_SUB
    out_row = lax.broadcasted_iota(jnp.int32, ((CONV_WIDTH - 1) * DN_SUB, span), 0)
    src_row = lax.broadcasted_iota(jnp.int32, ((CONV_WIDTH - 1) * DN_SUB, span), 1)
    back = (CONV_WIDTH - 1) - out_row // DN_SUB
    select = (src_row == DN_HALO + out_row % DN_SUB - back).astype(BF16)
    tiles = []
    for r in range(tb // DN_SUB):
        window = xp_ref[r * DN_SUB:r * DN_SUB + span, :]
        shifted = jnp.dot(select, window, preferred_element_type=F32)
        y = window[DN_HALO:, :].astype(F32) * convw[CONV_WIDTH - 1:CONV_WIDTH, :]
        for tap in range(CONV_WIDTH - 1):
            y = y + shifted[tap * DN_SUB:(tap + 1) * DN_SUB, :] * convw[tap:tap + 1, :]
        tiles.append(y)
    qkv = _silu(jnp.concatenate(tiles, axis=0))

    row = lax.broadcasted_iota(jnp.int32, (DN_SUB, DN_SUB), 0)
    col = lax.broadcasted_iota(jnp.int32, (DN_SUB, DN_SUB), 1)
    same_chunk = (row // CHUNK) == (col // CHUNK)
    lower_incl = same_chunk & (row >= col)
    strict_lower = same_chunk & (row > col)
    cum_lhs = jnp.concatenate([lower_incl.astype(F32), same_chunk.astype(F32)], axis=0)
    cum_rhs_t = (same_chunk & (row <= col)).astype(F32)

    ab = ab_ref[...]
    g_cols = -jnp.exp(prow_ref[0:1, :]) * _softplus(ab + prow_ref[1:2, :])
    g_rows = -jnp.exp(pcol_ref[:, 0:1]) * _softplus(abt_ref[...] + pcol_ref[:, 1:2])
    beta_cols = jax.nn.sigmoid(ab)

    nsub = tb // DN_SUB
    subs = [slice(s * DN_SUB, (s + 1) * DN_SUB) for s in range(nsub)]
    gl_rows, gcum_cols, gtot_cols, gcum_rows = [], [], [], []
    for rs in subs:
        cum = jnp.dot(cum_lhs, g_cols[rs, :], preferred_element_type=F32,
                      precision=lax.Precision.HIGHEST)
        gcum_cols.append(cum[:DN_SUB])
        gtot_cols.append(cum[DN_SUB:])
        gcum_rows.append(jnp.dot(g_rows[:, rs], cum_rhs_t, preferred_element_type=F32,
                                 precision=lax.Precision.HIGHEST))
        for c in range(DN_SUB // CHUNK):
            tot = jnp.exp(cum[DN_SUB + c * CHUNK:DN_SUB + c * CHUNK + 1, :])
            gl_rows.append(jnp.concatenate(
                [jnp.broadcast_to(tot[:, h:h + 1], (1, DN_HEAD_DIM)) for h in range(DN_HEADS)], axis=1))
    gl_ref[...] = jnp.concatenate(gl_rows, axis=0)

    chains = [(s, h) for s in range(nsub) for h in range(DN_HEADS)]

    def stack(fn):
        return jnp.stack([fn(s, h) for s, h in chains], axis=0)

    def head_cols(base):
        return stack(lambda s, h: qkv[subs[s], base + h * DN_HEAD_DIM:base + (h + 1) * DN_HEAD_DIM])

    q = head_cols(0)
    k = head_cols(DN_WIDTH)
    v = head_cols(2 * DN_WIDTH)
    gc = stack(lambda s, h: gcum_cols[s][:, h:h + 1])
    gt = stack(lambda s, h: gtot_cols[s][:, h:h + 1])
    gr = stack(lambda s, h: gcum_rows[s][h:h + 1, :])
    beta = stack(lambda s, h: beta_cols[subs[s], DN_HEADS + h:DN_HEADS + h + 1])

    q = q * lax.rsqrt(jnp.sum(q * q, axis=-1, keepdims=True) + 1e-6) * (DN_HEAD_DIM ** -0.5)
    k = k * lax.rsqrt(jnp.sum(k * k, axis=-1, keepdims=True) + 1e-6)
    lower3 = lower_incl[None]
    decay = jnp.where(lower3, jnp.exp(jnp.where(lower3, gc - gr, 0.0)), 0.0)
    k_beta = k * beta
    gram = _bmm_nt(jnp.concatenate([q, k_beta], axis=1).astype(BF16), k.astype(BF16))
    attn = jnp.where(lower3, gram[:, :DN_SUB] * decay, 0.0)
    lmat = jnp.where(strict_lower[None], gram[:, DN_SUB:] * decay, 0.0)
    tinv = _unit_lower_inverse(lmat, row, col)
    exp_gc = jnp.exp(gc)
    uw = _bmm(tinv.astype(BF16), jnp.concatenate([v * beta, k_beta * exp_gc], axis=2).astype(BF16))
    q_dec = (q * exp_gc).astype(BF16)
    k_end = k * jnp.exp(gt - gc)
    compact = attn[:, :, 0:CHUNK]
    for c in range(1, DN_SUB // CHUNK):
        compact = compact + attn[:, :, c * CHUNK:(c + 1) * CHUNK]
    compact = compact.astype(BF16)
    uw = uw.astype(BF16)

    for idx, (s, h) in enumerate(chains):
        rs = subs[s]
        hs = slice(h * DN_HEAD_DIM, (h + 1) * DN_HEAD_DIM)
        u_ref[rs, hs] = uw[idx, :, :DN_HEAD_DIM]
        w_ref[rs, hs] = uw[idx, :, DN_HEAD_DIM:]
        qd_ref[rs, hs] = q_dec[idx]
        ket_ref[0, hs, rs] = k_end[idx].T.astype(BF16)
        attn_ref[rs, h * CHUNK:(h + 1) * CHUNK] = compact[idx]


def _dn_scan_kernel(u_ref, w_ref, qd_ref, attn_ref, ket_ref, z_ref, gl_ref, normw_ref,
                    o_ref, state_ref):
    step = pl.program_id(0)

    @pl.when(step == 0)
    def _():
        state_ref[...] = jnp.zeros_like(state_ref)

    batch = u_ref.shape[0]
    normw = normw_ref[...]
    chains = [(b, h, slice(h * DN_HEAD_DIM, (h + 1) * DN_HEAD_DIM))
              for b in range(batch) for h in range(DN_HEADS)]
    states = [state_ref[b * DN_HEADS + h] for b, h, _ in chains]
    for c in range(DN_SCAN // CHUNK):
        rs = slice(c * CHUNK, (c + 1) * CHUNK)
        ws_qs = [jnp.dot(jnp.concatenate([w_ref[b, rs, hs], qd_ref[b, rs, hs]], axis=0),
                         state.astype(BF16), preferred_element_type=F32)
                 for (b, h, hs), state in zip(chains, states)]
        v_new = [(u_ref[b, rs, hs].astype(F32) - r[:CHUNK]).astype(BF16)
                 for (b, h, hs), r in zip(chains, ws_qs)]
        av_kv = [jnp.dot(jnp.concatenate([attn_ref[b, rs, h * CHUNK:(h + 1) * CHUNK],
                                          ket_ref[b, hs, rs]], axis=0),
                         vn, preferred_element_type=F32)
                 for (b, h, hs), vn in zip(chains, v_new)]
        states = [state * gl_ref[b, 0, c:c + 1, hs] + r[CHUNK:]
                  for (b, h, hs), state, r in zip(chains, states, av_kv)]
        for (b, h, hs), r1, r2 in zip(chains, ws_qs, av_kv):
            o = r1[CHUNK:] + r2[:CHUNK]
            o = o * lax.rsqrt(jnp.mean(o * o, axis=-1, keepdims=True) + RMS_EPS)
            o_ref[b, rs, hs] = (o * normw * _silu(z_ref[b, rs, hs].astype(F32))).astype(o_ref.dtype)
    for (b, h, _), state in zip(chains, states):
        state_ref[b * DN_HEADS + h] = state


def _deltanet(qkv, z, ab, abt, conv_w, a_log, dt_bias, dn_norm_w, batch, seq):
    t = batch * seq
    tb = min(DN_BLOCK, seq)
    blocks_per_seq = seq // tb
    prow = jnp.zeros((SUBLANES, LANES), F32).at[0, :DN_HEADS].set(a_log).at[1, :DN_HEADS].set(dt_bias)
    pcol = (jnp.zeros((SUBLANES, LANES), F32)
            .at[:DN_HEADS, 0].set(a_log).at[DN_HEADS:2 * DN_HEADS, 0].set(a_log)
            .at[:DN_HEADS, 1].set(dt_bias).at[DN_HEADS:2 * DN_HEADS, 1].set(dt_bias))

    def rows(width):
        return pl.BlockSpec((tb, width), lambda i: (i, 0))

    def const(shape):
        return pl.BlockSpec(shape, lambda i: (0,) * len(shape))

    u, w, qd, attn, ket, gl = pl.pallas_call(
        functools.partial(_dn_prep_kernel, blocks_per_seq=blocks_per_seq),
        grid=(t // tb,),
        in_specs=[rows(3 * DN_WIDTH),
                  pl.BlockSpec((DN_HALO, 3 * DN_WIDTH),
                               lambda i: (jnp.maximum(i * (tb // DN_HALO) - 1, 0), 0)),
                  rows(LANES), pl.BlockSpec((SUBLANES, tb), lambda i: (0, i)),
                  const((CONV_WIDTH, 3 * DN_WIDTH)), const((SUBLANES, LANES)), const((SUBLANES, LANES))],
        out_specs=[rows(DN_WIDTH), rows(DN_WIDTH), rows(DN_WIDTH), rows(DN_HEADS * CHUNK),
                   pl.BlockSpec((1, DN_WIDTH, tb), lambda i: (i // blocks_per_seq, 0, i % blocks_per_seq)),
                   pl.BlockSpec((tb // CHUNK, DN_WIDTH), lambda i: (i, 0))],
        out_shape=[jax.ShapeDtypeStruct((t, DN_WIDTH), BF16)] * 3
        + [jax.ShapeDtypeStruct((t, DN_HEADS * CHUNK), BF16),
           jax.ShapeDtypeStruct((batch, DN_WIDTH, seq), BF16),
           jax.ShapeDtypeStruct((t // CHUNK, DN_WIDTH), F32)],
        scratch_shapes=[pltpu.VMEM((DN_HALO + tb, 3 * DN_WIDTH), BF16)],
        compiler_params=_cparams(("parallel",)),
        name="deltanet_prep",
    )(qkv, qkv, ab, abt, conv_w, prow, pcol)

    def seq_rows(width):
        return pl.BlockSpec((batch, DN_SCAN, width), lambda c: (0, c, 0))

    def view(a):
        return a.reshape(batch, seq, a.shape[-1])

    o = pl.pallas_call(
        _dn_scan_kernel,
        grid=(seq // DN_SCAN,),
        in_specs=[seq_rows(DN_WIDTH), seq_rows(DN_WIDTH), seq_rows(DN_WIDTH), seq_rows(DN_HEADS * CHUNK),
                  pl.BlockSpec((batch, DN_WIDTH, DN_SCAN), lambda c: (0, 0, c)),
                  seq_rows(DN_WIDTH),
                  pl.BlockSpec((batch, 1, DN_SCAN // CHUNK, DN_WIDTH), lambda c: (0, c, 0, 0)),
                  pl.BlockSpec((1, DN_HEAD_DIM), lambda c: (0, 0))],
        out_specs=seq_rows(DN_WIDTH),
        out_shape=jax.ShapeDtypeStruct((batch, seq, DN_WIDTH), BF16),
        scratch_shapes=[pltpu.VMEM((batch * DN_HEADS, DN_HEAD_DIM, DN_HEAD_DIM), F32)],
        compiler_params=_cparams(("arbitrary",)),
        name="deltanet_scan",
    )(view(u), view(w), view(qd), view(attn), ket, view(z),
      gl.reshape(batch, seq // DN_SCAN, DN_SCAN // CHUNK, DN_WIDTH), dn_norm_w.reshape(1, DN_HEAD_DIM))
    return o.reshape(t, DN_WIDTH)


def _swa_kernel(q_ref, kvc_ref, kvp_ref, sink_ref, o_ref):
    qi = lax.broadcasted_iota(jnp.int32, (WINDOW, 2 * WINDOW), 0)
    kj = lax.broadcasted_iota(jnp.int32, (WINDOW, 2 * WINDOW), 1)
    in_window = (kj > qi) & (kj <= qi + WINDOW)
    first_valid = in_window & ((kj >= WINDOW) | (pl.program_id(1) > 0))
    kv = jnp.concatenate([kvp_ref[...], kvc_ref[...]], axis=0)
    scores = []
    for j in range(SWA_BLOCKS):
        band = slice(j * WINDOW, (j + 2) * WINDOW)
        valid = first_valid if j == 0 else in_window
        for hq in range(SW_Q_HEADS):
            hk = hq // SW_GROUP
            q = q_ref[j * WINDOW:(j + 1) * WINDOW, hq * SW_HEAD_DIM:(hq + 1) * SW_HEAD_DIM]
            s = _bdot_nt(q, kv[band, hk * SW_HEAD_DIM:(hk + 1) * SW_HEAD_DIM])
            scores.append(jnp.where(valid, s, -jnp.inf))
    s = jnp.concatenate(scores, axis=0)
    sink = sink_ref[...]
    m = jnp.maximum(jnp.max(s, axis=-1, keepdims=True), sink)
    p = jnp.exp(s - m)
    inv_denom = 1.0 / (jnp.sum(p, axis=-1, keepdims=True) + jnp.exp(sink - m))
    p = p.astype(BF16)
    for j in range(SWA_BLOCKS):
        band = slice(j * WINDOW, (j + 2) * WINDOW)
        outs = []
        for hq in range(SW_Q_HEADS):
            hk = hq // SW_GROUP
            rs = slice((j * SW_Q_HEADS + hq) * WINDOW, (j * SW_Q_HEADS + hq + 1) * WINDOW)
            v = kv[band, SW_KV_WIDTH + hk * SW_HEAD_DIM:SW_KV_WIDTH + (hk + 1) * SW_HEAD_DIM]
            outs.append(jnp.dot(p[rs], v, preferred_element_type=F32) * inv_denom[rs])
        o_ref[j * WINDOW:(j + 1) * WINDOW, :] = jnp.concatenate(outs, axis=1).astype(o_ref.dtype)


def _swa(qsw, kvsw, sinks, batch, seq):
    t = batch * seq
    tq = SWA_BLOCKS * WINDOW
    nstep = seq // tq
    sink_rows = jnp.tile(jnp.repeat(sinks.astype(F32), WINDOW), SWA_BLOCKS).reshape(-1, 1)
    return pl.pallas_call(
        _swa_kernel,
        grid=(batch, nstep),
        in_specs=[pl.BlockSpec((tq, SW_WIDTH), lambda b, i: (b * nstep + i, 0)),
                  pl.BlockSpec((tq, 2 * SW_KV_WIDTH), lambda b, i: (b * nstep + i, 0)),
                  pl.BlockSpec((WINDOW, 2 * SW_KV_WIDTH),
                               lambda b, i: ((b * nstep + i) * SWA_BLOCKS - jnp.minimum(i, 1), 0)),
                  pl.BlockSpec((SWA_BLOCKS * SW_Q_HEADS * WINDOW, 1), lambda b, i: (0, 0))],
        out_specs=pl.BlockSpec((tq, SW_WIDTH), lambda b, i: (b * nstep + i, 0)),
        out_shape=jax.ShapeDtypeStruct((t, SW_WIDTH), BF16),
        compiler_params=_cparams(("parallel", "parallel")),
        name="sliding_window_attn",
    )(qsw, kvsw, kvsw, sink_rows)


def _outproj_kernel(odn_ref, osw_ref, w_ref, x_ref, g_ref, b_ref, o_ref):
    mix = (jnp.dot(odn_ref[...], w_ref[0, 0:DN_WIDTH, :], preferred_element_type=F32)
           + jnp.dot(osw_ref[...], w_ref[0, DN_WIDTH:, :], preferred_element_type=F32))
    o_ref[...] = _layer_norm(ALPHA * x_ref[...] + mix, g_ref[...], b_ref[...])


def _outproj_ln(o_dn, o_sw, w_out_bf16, layer, x2d, g, b):
    t = x2d.shape[0]
    tm = min(TM_OUT, t)

    def rows(width):
        return pl.BlockSpec((tm, width), lambda i: (i, 0))

    vec = pl.BlockSpec((1, D_MODEL), lambda i: (0, 0))
    return pl.pallas_call(
        _outproj_kernel,
        grid=(t // tm,),
        in_specs=[rows(DN_WIDTH), rows(SW_WIDTH),
                  pl.BlockSpec((1, DN_WIDTH + SW_WIDTH, D_MODEL), lambda i: (layer, 0, 0)),
                  rows(D_MODEL), vec, vec],
        out_specs=rows(D_MODEL),
        out_shape=jax.ShapeDtypeStruct((t, D_MODEL), F32),
        compiler_params=_cparams(("parallel",)),
        name="out_proj_ln",
    )(o_dn, o_sw, w_out_bf16, x2d, g.reshape(1, D_MODEL), b.reshape(1, D_MODEL))


def _ffn_kernel(x_ref, wg_ref, wu_ref, wd_ref, g_ref, b_ref, o_ref, xb_ref, acc_ref):
    f = pl.program_id(1)

    @pl.when(f == 0)
    def _():
        xb_ref[...] = x_ref[...].astype(BF16)
        acc_ref[...] = jnp.zeros_like(acc_ref)

    xb = xb_ref[...]
    gate = jnp.dot(xb, wg_ref[0].astype(BF16), preferred_element_type=F32)
    up = jnp.dot(xb, wu_ref[0].astype(BF16), preferred_element_type=F32)
    hid = (_silu(gate) * up).astype(BF16)
    acc_ref[...] += jnp.dot(hid, wd_ref[0].astype(BF16), preferred_element_type=F32)

    @pl.when(f == pl.num_programs(1) - 1)
    def _():
        o_ref[...] = _layer_norm(ALPHA * x_ref[...] + acc_ref[...], g_ref[...], b_ref[...])


def _ffn_ln(x2d, w_gate, w_up, w_down, idx, g, b):
    t = x2d.shape[0]
    tm = min(TM_FFN, t)
    vec = pl.BlockSpec((1, D_MODEL), lambda i, f: (0, 0))
    return pl.pallas_call(
        _ffn_kernel,
        grid=(t // tm, D_FF // TF_FFN),
        in_specs=[pl.BlockSpec((tm, D_MODEL), lambda i, f: (i, 0)),
                  pl.BlockSpec((1, D_MODEL, TF_FFN), lambda i, f: (idx, 0, f)),
                  pl.BlockSpec((1, D_MODEL, TF_FFN), lambda i, f: (idx, 0, f)),
                  pl.BlockSpec((1, TF_FFN, D_MODEL), lambda i, f: (idx, f, 0)),
                  vec, vec],
        out_specs=pl.BlockSpec((tm, D_MODEL), lambda i, f: (i, 0)),
        out_shape=jax.ShapeDtypeStruct((t, D_MODEL), F32),
        scratch_shapes=[pltpu.VMEM((tm, D_MODEL), BF16), pltpu.VMEM((tm, D_MODEL), F32)],
        compiler_params=_cparams(("parallel", "arbitrary")),
        name="dense_swiglu_ln",
    )(x2d, w_gate, w_up, w_down, g.reshape(1, D_MODEL), b.reshape(1, D_MODEL))


R_E1, R_E2, R_RANK1, R_RANK2, R_G1, R_G2 = range(6)


def _router_kernel(x_ref, wr_ref, meta_ref, count_ref, carry_ref):
    i = pl.program_id(0)

    @pl.when(i == 0)
    def _():
        carry_ref[...] = jnp.zeros_like(carry_ref)

    tm = x_ref.shape[0]
    logits = jnp.dot(x_ref[...], wr_ref[...], preferred_element_type=F32,
                     precision=lax.Precision.HIGHEST)
    lane = lax.broadcasted_iota(jnp.int32, logits.shape, 1)
    logits = jnp.where(lane < N_EXPERTS, logits, -jnp.inf)
    m1 = jnp.max(logits, axis=-1, keepdims=True)
    e1 = jnp.min(jnp.where(logits == m1, lane, LANES), axis=-1, keepdims=True)
    rest = jnp.where(lane == e1, -jnp.inf, logits)
    m2 = jnp.max(rest, axis=-1, keepdims=True)
    e2 = jnp.min(jnp.where(rest == m2, lane, LANES), axis=-1, keepdims=True)
    ex = jnp.exp(m2 - m1)
    g1 = 1.0 / (1.0 + ex)
    g2 = ex / (1.0 + ex)

    hit = ((lane == e1) | (lane == e2)).astype(BF16)
    r = lax.broadcasted_iota(jnp.int32, (tm, tm), 0)
    c = lax.broadcasted_iota(jnp.int32, (tm, tm), 1)
    before = jnp.dot((c < r).astype(BF16), hit, preferred_element_type=F32) + carry_ref[0:1, :]
    rank1 = jnp.sum(jnp.where(lane == e1, before, 0.0), axis=-1, keepdims=True)
    rank2 = jnp.sum(jnp.where(lane == e2, before, 0.0), axis=-1, keepdims=True)
    carry_ref[0:1, :] = carry_ref[0:1, :] + jnp.sum(hit.astype(F32), axis=0, keepdims=True)
    count_ref[...] = jnp.broadcast_to(carry_ref[0:1, :], count_ref.shape)

    meta = jnp.zeros((tm, SUBLANES), F32)
    lane8 = lax.broadcasted_iota(jnp.int32, (tm, SUBLANES), 1)
    for idx, val in ((R_E1, e1.astype(F32)), (R_E2, e2.astype(F32)), (R_RANK1, rank1),
                     (R_RANK2, rank2), (R_G1, g1), (R_G2, g2)):
        meta = jnp.where(lane8 == idx, val, meta)
    meta_ref[...] = meta


def _router(x2d, router_w):
    t = x2d.shape[0]
    tm = min(TM_ROUTE, t)
    wr = jnp.zeros((D_MODEL, LANES), F32).at[:, :N_EXPERTS].set(router_w)
    return pl.pallas_call(
        _router_kernel,
        grid=(t // tm,),
        in_specs=[pl.BlockSpec((tm, D_MODEL), lambda i: (i, 0)),
                  pl.BlockSpec((D_MODEL, LANES), lambda i: (0, 0))],
        out_specs=[pl.BlockSpec((tm, SUBLANES), lambda i: (i, 0)),
                   pl.BlockSpec((SUBLANES, LANES), lambda i: (0, 0))],
        out_shape=[jax.ShapeDtypeStruct((t, SUBLANES), F32),
                   jax.ShapeDtypeStruct((SUBLANES, LANES), F32)],
        scratch_shapes=[pltpu.VMEM((SUBLANES, LANES), F32)],
        compiler_params=_cparams(("arbitrary",)),
        name="moe_router",
    )(x2d, wr)


def _row_copy(src_ref, src_row, dst_ref, dst_row, sem):
    return pltpu.make_async_copy(src_ref.at[pl.ds(src_row, 1), :], dst_ref.at[pl.ds(dst_row, 1), :], sem)


def _dispatch_kernel(dest1_ref, dest2_ref, x_ref, xs_ref, sem):
    base = pl.program_id(0) * TM_MOVE

    def issue(j, carry):
        tok = base + j
        _row_copy(x_ref, j, xs_ref, dest1_ref[tok], sem).start(priority=0)
        _row_copy(x_ref, j, xs_ref, dest2_ref[tok], sem).start(priority=1)
        return carry

    lax.fori_loop(0, TM_MOVE, issue, 0, unroll=MOVE_UNROLL)
    for _ in range(2):
        pltpu.make_async_copy(x_ref, xs_ref.at[pl.ds(0, TM_MOVE), :], sem).wait()


def _dispatch(x2d, dest1, dest2, rows_sorted):
    t = x2d.shape[0]
    return pl.pallas_call(
        _dispatch_kernel,
        grid_spec=pltpu.PrefetchScalarGridSpec(
            num_scalar_prefetch=2,
            grid=(t // TM_MOVE,),
            in_specs=[pl.BlockSpec((TM_MOVE, D_MODEL), lambda i, d1, d2: (i, 0))],
            out_specs=pl.BlockSpec(memory_space=pl.ANY),
            scratch_shapes=[pltpu.SemaphoreType.DMA(())]),
        out_shape=jax.ShapeDtypeStruct((rows_sorted, D_MODEL), F32),
        compiler_params=_cparams(("arbitrary",)),
        name="moe_dispatch",
    )(dest1, dest2, x2d)


def _expert_kernel(vtile_ref, vexpert_ref, vfidx_ref, vlo_ref, vhi_ref, vfirst_ref, vlast_ref,
                   xs_ref, wg_ref, wu_ref, wd_ref, ys_ref, xb_ref, acc_ref, wgb_ref, wub_ref, wdb_ref):
    v = pl.program_id(0)
    f = pl.program_id(1)
    lo = vlo_ref[v]
    hi = vhi_ref[v]
    whole = (lo == 0) & (hi == TM_EXPERT)

    def swiglu_rows(rows, wg, wu, wd):
        xb = xb_ref[rows, :]
        gate = jnp.dot(xb, wg, preferred_element_type=F32)
        up = jnp.dot(xb, wu, preferred_element_type=F32)
        hid = (_silu(gate) * up).astype(BF16)
        acc_ref[rows, :] += jnp.dot(hid, wd, preferred_element_type=F32)

    @pl.when(hi > lo)
    def _():
        @pl.when(f == 0)
        def _():
            rowid = lax.broadcasted_iota(jnp.int32, xs_ref.shape, 0)
            xb_ref[...] = jnp.where((rowid >= lo) & (rowid < hi), xs_ref[...], 0.0).astype(BF16)

        @pl.when((f == 0) & (vfirst_ref[v] == 1))
        def _():
            acc_ref[...] = jnp.zeros_like(acc_ref)

        @pl.when(whole)
        def _():
            swiglu_rows(slice(None), wg_ref[0, 0].astype(BF16), wu_ref[0, 0].astype(BF16),
                        wd_ref[0, 0].astype(BF16))

        @pl.when(jnp.logical_not(whole))
        def _():
            wgb_ref[...] = wg_ref[0, 0].astype(BF16)
            wub_ref[...] = wu_ref[0, 0].astype(BF16)
            wdb_ref[...] = wd_ref[0, 0].astype(BF16)
            for sb in range(TM_EXPERT // TM_EXPERT_SUB):
                @pl.when((sb * TM_EXPERT_SUB < hi) & ((sb + 1) * TM_EXPERT_SUB > lo))
                def _():
                    swiglu_rows(slice(sb * TM_EXPERT_SUB, (sb + 1) * TM_EXPERT_SUB),
                                wgb_ref[...], wub_ref[...], wdb_ref[...])

        @pl.when((f == pl.num_programs(1) - 1) & (vlast_ref[v] == 1))
        def _():
            ys_ref[...] = acc_ref[...]


def _expert_visits(counts, n_tiles):
    n_visits = n_tiles + N_EXPERTS - 1
    ends = jnp.cumsum(counts)
    offs = ends - counts
    first_tile = offs // TM_EXPERT
    per_expert = jnp.where(counts > 0, (ends - 1) // TM_EXPERT - first_tile + 1, 0)
    vend = jnp.cumsum(per_expert)
    vstart = vend - per_expert
    total = vend[-1]
    vid = jnp.arange(n_visits, dtype=jnp.int32)
    real = vid < total
    vid_c = jnp.minimum(vid, jnp.maximum(total - 1, 0))
    vexpert = jnp.minimum(jnp.sum(vid_c[:, None] >= vend[None, :], axis=1), N_EXPERTS - 1)
    vtile = first_tile[vexpert] + vid_c - vstart[vexpert]
    lo = jnp.maximum(offs[vexpert], vtile * TM_EXPERT) - vtile * TM_EXPERT
    hi = jnp.minimum(ends[vexpert], (vtile + 1) * TM_EXPERT) - vtile * TM_EXPERT
    prev_tile = jnp.concatenate([jnp.full((1,), -1, vtile.dtype), vtile[:-1]])
    next_tile = jnp.concatenate([vtile[1:], jnp.full((1,), -1, vtile.dtype)])
    first = real & (vtile != prev_tile)
    last = real & ((vtile != next_tile) | (vid == total - 1))
    i32 = lambda a: a.astype(jnp.int32)
    fpin = jnp.where(real, -1, D_FF // TF_FFN - 1)
    return (i32(vtile), i32(vexpert), i32(fpin), i32(jnp.where(real, lo, 0)),
            i32(jnp.where(real, hi, 0)), i32(first), i32(last), i32(offs))


def _expert_ffn(xs, visits, w_gate, w_up, w_down, idx):
    rows_sorted = xs.shape[0]
    n_visits = visits[0].shape[0]
    nf = D_FF // TF_FFN

    def f_idx(v, f, vf):
        return jnp.where(vf[v] >= 0, vf[v], f)

    def rows_map(v, f, vt, ve, vf, *_):
        return (vt[v], 0)

    def wcol_map(v, f, vt, ve, vf, *_):
        return (idx, ve[v], 0, f_idx(v, f, vf))

    def wrow_map(v, f, vt, ve, vf, *_):
        return (idx, ve[v], f_idx(v, f, vf), 0)

    return pl.pallas_call(
        _expert_kernel,
        grid_spec=pltpu.PrefetchScalarGridSpec(
            num_scalar_prefetch=len(visits),
            grid=(n_visits, nf),
            in_specs=[pl.BlockSpec((TM_EXPERT, D_MODEL), rows_map),
                      pl.BlockSpec((1, 1, D_MODEL, TF_FFN), wcol_map),
                      pl.BlockSpec((1, 1, D_MODEL, TF_FFN), wcol_map),
                      pl.BlockSpec((1, 1, TF_FFN, D_MODEL), wrow_map)],
            out_specs=pl.BlockSpec((TM_EXPERT, D_MODEL), rows_map),
            scratch_shapes=[pltpu.VMEM((TM_EXPERT, D_MODEL), BF16),
                            pltpu.VMEM((TM_EXPERT, D_MODEL), F32),
                            pltpu.VMEM((D_MODEL, TF_FFN), BF16),
                            pltpu.VMEM((D_MODEL, TF_FFN), BF16),
                            pltpu.VMEM((TF_FFN, D_MODEL), BF16)]),
        out_shape=jax.ShapeDtypeStruct((rows_sorted, D_MODEL), F32),
        compiler_params=_cparams(("arbitrary", "arbitrary")),
        name="moe_expert_swiglu",
    )(*visits, xs, w_gate, w_up, w_down)


def _combine_kernel(dest1_ref, dest2_ref, ys_ref, x_ref, meta_ref, g_ref, b_ref, o_ref,
                    buf1_ref, buf2_ref, sem):
    base = pl.program_id(0) * TM_MOVE

    def issue(j, carry):
        tok = base + j
        _row_copy(ys_ref, dest1_ref[tok], buf1_ref, j, sem).start(priority=0)
        _row_copy(ys_ref, dest2_ref[tok], buf2_ref, j, sem).start(priority=1)
        return carry

    lax.fori_loop(0, TM_MOVE, issue, 0, unroll=MOVE_UNROLL)
    for buf_ref in (buf1_ref, buf2_ref):
        pltpu.make_async_copy(ys_ref.at[pl.ds(0, TM_MOVE), :], buf_ref, sem).wait()

    meta = meta_ref[...]
    g1 = meta[:, R_G1:R_G1 + 1]
    g2 = meta[:, R_G2:R_G2 + 1]
    moe = g1 * buf1_ref[...] + g2 * buf2_ref[...]
    o_ref[...] = _layer_norm(ALPHA * x_ref[...] + moe, g_ref[...], b_ref[...])


def _combine_ln(ys, dest1, dest2, x2d, meta, g, b):
    t = x2d.shape[0]
    vec = pl.BlockSpec((1, D_MODEL), lambda i, d1, d2: (0, 0))
    return pl.pallas_call(
        _combine_kernel,
        grid_spec=pltpu.PrefetchScalarGridSpec(
            num_scalar_prefetch=2,
            grid=(t // TM_MOVE,),
            in_specs=[pl.BlockSpec(memory_space=pl.ANY),
                      pl.BlockSpec((TM_MOVE, D_MODEL), lambda i, d1, d2: (i, 0)),
                      pl.BlockSpec((TM_MOVE, SUBLANES), lambda i, d1, d2: (i, 0)),
                      vec, vec],
            out_specs=pl.BlockSpec((TM_MOVE, D_MODEL), lambda i, d1, d2: (i, 0)),
            scratch_shapes=[pltpu.VMEM((TM_MOVE, D_MODEL), F32),
                            pltpu.VMEM((TM_MOVE, D_MODEL), F32),
                            pltpu.SemaphoreType.DMA(())]),
        out_shape=jax.ShapeDtypeStruct((t, D_MODEL), F32),
        compiler_params=_cparams(("arbitrary",)),
        name="moe_combine_ln",
    )(dest1, dest2, ys, x2d, meta, g.reshape(1, D_MODEL), b.reshape(1, D_MODEL))


def _moe_ln(x2d, router_w, w_gate, w_up, w_down, idx, g, b):
    t = x2d.shape[0]
    meta, counts = _router(x2d, router_w)
    counts = counts[0, :N_EXPERTS].astype(jnp.int32)
    *visits, group_start = _expert_visits(counts, (2 * t) // TM_EXPERT)
    e1 = meta[:, R_E1].astype(jnp.int32)
    e2 = meta[:, R_E2].astype(jnp.int32)
    dest1 = group_start[e1] + meta[:, R_RANK1].astype(jnp.int32)
    dest2 = group_start[e2] + meta[:, R_RANK2].astype(jnp.int32)
    xs = _dispatch(x2d, dest1, dest2, 2 * t)
    ys = _expert_ffn(xs, tuple(visits), w_gate, w_up, w_down, idx)
    return _combine_ln(ys, dest1, dest2, x2d, meta, g, b)


def _pack_w_in(w):
    dn = 4 * DN_WIDTH
    ab = w[..., dn:dn + 2 * DN_HEADS]
    pad = jnp.zeros(w.shape[:-1] + (LANES - 2 * DN_HEADS,), w.dtype)
    packed = jnp.concatenate([w[..., :dn], w[..., dn + 2 * DN_HEADS:], ab, pad], axis=-1).astype(BF16)
    return packed, jnp.swapaxes(ab, -1, -2).astype(BF16)


@jax.jit
def _trunk(x, positions, w_in, conv_w, a_log, dt_bias, dn_norm_w, sinks, w_out, ln_g, ln_b,
           ffn_w_gate, ffn_w_up, ffn_w_down, router_w, moe_w_gate, moe_w_up, moe_w_down):
    batch, seq, _ = x.shape
    x2d = x.reshape(batch * seq, D_MODEL)
    cos, sin = _rope_tables(positions)
    w_packed, wabt = _pack_w_in(w_in)
    w_out_bf16 = w_out.astype(BF16)
    for layer in range(DEPTH):
        qkv, z, qsw, kvsw, ab, abt = _inproj(x2d, w_packed, wabt, layer, cos, sin)
        o_dn = _deltanet(qkv, z, ab, abt, conv_w[layer], a_log[layer], dt_bias[layer],
                         dn_norm_w[layer], batch, seq)
        o_sw = _swa(qsw, kvsw, sinks[layer], batch, seq)
        x2d = _outproj_ln(o_dn, o_sw, w_out_bf16, layer, x2d, ln_g[layer, 0], ln_b[layer, 0])
        i = layer // 2
        if layer % 2 == 0:
            x2d = _ffn_ln(x2d, ffn_w_gate, ffn_w_up, ffn_w_down, i, ln_g[layer, 1], ln_b[layer, 1])
        else:
            x2d = _moe_ln(x2d, router_w[i], moe_w_gate, moe_w_up, moe_w_down, i,
                          ln_g[layer, 1], ln_b[layer, 1])
    return x2d.reshape(batch, seq, D_MODEL)


def kernel(x, positions, w_in, conv_w, a_log, dt_bias, dn_norm_w, sinks, w_out, ln_g, ln_b,
           ffn_w_gate, ffn_w_up, ffn_w_down, router_w, moe_w_gate, moe_w_up, moe_w_down):
    return _trunk(x, positions, w_in, conv_w, a_log, dt_bias, dn_norm_w, sinks, w_out, ln_g, ln_b,
                  ffn_w_gate, ffn_w_up, ffn_w_down, router_w, moe_w_gate, moe_w_up, moe_w_down)
```

```python
import functools

import jax
import jax.numpy as jnp
import numpy as np
from jax import lax
from jax.experimental import pallas as pl
from jax.experimental.pallas import tpu as pltpu

F32 = jnp.float32
BF16 = jnp.bfloat16

D_MODEL = 1024
DEPTH = 4
DN_HEADS = 4
DN_HEAD_DIM = 128
DN_WIDTH = DN_HEADS * DN_HEAD_DIM
CONV_WIDTH = 4
CHUNK = 64
SW_Q_HEADS = 8
SW_KV_HEADS = 2
SW_GROUP = SW_Q_HEADS // SW_KV_HEADS
SW_HEAD_DIM = 64
SW_WIDTH = SW_Q_HEADS * SW_HEAD_DIM
SW_KV_WIDTH = SW_KV_HEADS * SW_HEAD_DIM
WINDOW = 128
ROPE_THETA = 10000.0
D_FF = 3584
N_EXPERTS = 8
ALPHA = (2.0 * DEPTH) ** 0.25
LN_EPS = 1e-5
RMS_EPS = 1e-6

LANES = 128
SUBLANES = 8
DN_HALO = 16
DN_BLOCK = 512
DN_SUB = 128
DN_SCAN = 128
SWA_BLOCKS = 4
VMEM_LIMIT = 56 * 1024 * 1024

C_QKV = (0, 3 * DN_WIDTH)
C_Z = (C_QKV[1], C_QKV[1] + DN_WIDTH)
C_QSW = (C_Z[1], C_Z[1] + SW_WIDTH)
C_KVSW = (C_QSW[1], C_QSW[1] + 2 * SW_KV_WIDTH)
C_AB = (C_KVSW[1], C_KVSW[1] + LANES)
IN_COLS_PACKED = C_AB[1]

TM_PROJ = 512
TM_ROPE = 2048
TM_FFN = 1024
TF_FFN = 512
TM_ROUTE = 1024
TM_EXPERT = 1024
TM_EXPERT_SUB = 256
TM_OUT = 1024
TM_MOVE = 512
MOVE_UNROLL = 8


def _cparams(sem):
    return pltpu.CompilerParams(dimension_semantics=sem, vmem_limit_bytes=VMEM_LIMIT)


def _bdot(a, b):
    return jnp.dot(a.astype(BF16), b.astype(BF16), preferred_element_type=F32)


def _bdot_nt(a, b):
    return lax.dot_general(a.astype(BF16), b.astype(BF16), (((1,), (1,)), ((), ())),
                           preferred_element_type=F32)


def _silu(x):
    return x * jax.nn.sigmoid(x)


def _softplus(x):
    return jnp.maximum(x, 0.0) + jnp.log1p(jnp.exp(-jnp.abs(x)))


def _layer_norm(y, g, b):
    mu = jnp.mean(y, axis=-1, keepdims=True)
    d = y - mu
    var = jnp.mean(d * d, axis=-1, keepdims=True)
    return d * lax.rsqrt(var + LN_EPS) * g + b


def _rope_table_kernel(pos_ref, invf_ref, sign_ref, cos_ref, sin_ref):
    ang = pos_ref[...].astype(F32) * invf_ref[...]
    cos_ref[...] = jnp.cos(ang)
    sin_ref[...] = jnp.sin(ang) * sign_ref[...]


def _rope_tables(positions):
    t = positions.size
    half = SW_HEAD_DIM // 2
    inv_freq = ROPE_THETA ** (-jnp.arange(0, SW_HEAD_DIM, 2, dtype=F32) / SW_HEAD_DIM)
    reps = LANES // half
    invf = jnp.tile(inv_freq, reps).reshape(1, LANES)
    sign = jnp.tile(jnp.concatenate([-jnp.ones((half,), F32), jnp.ones((half,), F32)]),
                    LANES // SW_HEAD_DIM).reshape(1, LANES)
    tm = min(TM_ROPE, t)
    row = pl.BlockSpec((1, LANES), lambda i: (0, 0))
    tab = pl.BlockSpec((tm, LANES), lambda i: (i, 0))
    return pl.pallas_call(
        _rope_table_kernel,
        grid=(t // tm,),
        in_specs=[pl.BlockSpec((tm, 1), lambda i: (i, 0)), row, row],
        out_specs=[tab, tab],
        out_shape=[jax.ShapeDtypeStruct((t, LANES), F32)] * 2,
        compiler_params=_cparams(("parallel",)),
        name="rope_tables",
    )(positions.reshape(t, 1), invf, sign)


def _rope(x, cos, sin_signed):
    width = x.shape[-1]
    half = SW_HEAD_DIM // 2
    lane = lax.broadcasted_iota(jnp.int32, x.shape, 1)
    first = (lane % SW_HEAD_DIM) < half
    partner = jnp.where(first, pltpu.roll(x, width - half, 1), pltpu.roll(x, half, 1))
    return x * cos + partner * sin_signed


def _inproj_kernel(x_ref, w_ref, wabt_ref, cos_ref, sin_ref,
                   qkv_ref, z_ref, qsw_ref, ktsw_ref, vsw_ref, ab_ref, abt_ref):
    xb = x_ref[...].astype(BF16)

    def mm(cols):
        return jnp.dot(xb, w_ref[0, :, cols[0]:cols[1]], preferred_element_type=F32)

    qkv_ref[...] = mm(C_QKV).astype(BF16)
    z_ref[...] = mm(C_Z).astype(BF16)
    cos = cos_ref[...]
    sin = sin_ref[...]
    reps = SW_WIDTH // LANES
    q = _rope(mm(C_QSW), jnp.tile(cos, (1, reps)), jnp.tile(sin, (1, reps))) * (SW_HEAD_DIM ** -0.5)
    qsw_ref[...] = q.astype(BF16)
    kv = mm(C_KVSW)
    ktsw_ref[...] = _rope(kv[:, :SW_KV_WIDTH], cos, sin).T.astype(BF16)
    vsw_ref[...] = kv[:, SW_KV_WIDTH:].astype(BF16)
    ab_ref[...] = mm(C_AB)
    abt_ref[...] = lax.dot_general(wabt_ref[0], xb, (((1,), (1,)), ((), ())),
                                   preferred_element_type=F32)


def _inproj(x2d, w_packed, wabt, layer, cos, sin):
    t = x2d.shape[0]
    tm = min(TM_PROJ, t)

    def rows(width):
        return pl.BlockSpec((tm, width), lambda i: (i, 0))

    return pl.pallas_call(
        _inproj_kernel,
        grid=(t // tm,),
        in_specs=[rows(D_MODEL),
                  pl.BlockSpec((1, D_MODEL, IN_COLS_PACKED), lambda i: (layer, 0, 0)),
                  pl.BlockSpec((1, SUBLANES, D_MODEL), lambda i: (layer, 0, 0)),
                  rows(LANES), rows(LANES)],
        out_specs=[rows(3 * DN_WIDTH), rows(DN_WIDTH), rows(SW_WIDTH),
                   pl.BlockSpec((SW_KV_WIDTH, tm), lambda i: (0, i)), rows(SW_KV_WIDTH),
                   rows(LANES), pl.BlockSpec((SUBLANES, tm), lambda i: (0, i))],
        out_shape=[jax.ShapeDtypeStruct((t, 3 * DN_WIDTH), BF16),
                   jax.ShapeDtypeStruct((t, DN_WIDTH), BF16),
                   jax.ShapeDtypeStruct((t, SW_WIDTH), BF16),
                   jax.ShapeDtypeStruct((SW_KV_WIDTH, t), BF16),
                   jax.ShapeDtypeStruct((t, SW_KV_WIDTH), BF16),
                   jax.ShapeDtypeStruct((t, LANES), F32),
                   jax.ShapeDtypeStruct((SUBLANES, t), F32)],
        compiler_params=_cparams(("parallel",)),
        name="in_proj",
    )(x2d, w_packed, wabt, cos, sin)


def _bmm(a, b):
    return lax.dot_general(a, b, (((2,), (1,)), ((0,), (0,))), preferred_element_type=F32)


def _bmm_nt(a, b):
    return lax.dot_general(a, b, (((2,), (2,)), ((0,), (0,))), preferred_element_type=F32)


def _unit_lower_inverse(lmat, row, col):
    same16 = ((row // 16) == (col // 16))[None]
    same32 = ((row // 32) == (col // 32))[None]
    eye = (row == col).astype(F32)[None]
    ld = jnp.where(same16, lmat, 0.0)
    inv = eye - ld
    power = ld.astype(BF16)
    for _ in range(3):
        power = _bmm(power, power).astype(BF16)
        inv = inv + _bmm(inv.astype(BF16), power)
    lmat_bf16 = lmat.astype(BF16)
    for off_diag in (same32 & ~same16, ~same32):
        c = jnp.where(off_diag, lmat_bf16, jnp.zeros_like(lmat_bf16))
        inv_bf16 = inv.astype(BF16)
        inv = inv - _bmm(_bmm(inv_bf16, c).astype(BF16), inv_bf16)
    return inv


def _dn_prep_kernel(qkv_ref, halo_ref, ab_ref, abt_ref, convw_ref, prow_ref, pcol_ref,
                    u_ref, w_ref, qd_ref, attn_ref, ket_ref, gl_ref, xp_ref, *, blocks_per_seq):
    tb = qkv_ref.shape[0]
    first = (pl.program_id(0) % blocks_per_seq) == 0

    halo = halo_ref[...]
    xp_ref[0:DN_HALO, :] = jnp.where(first, jnp.zeros_like(halo), halo)
    xp_ref[DN_HALO:DN_HALO + tb, :] = qkv_ref[...]
    convw = convw_ref[...]
    span = DN_HALO + DN_SUB
    out_row = lax.broadcasted_iota(jnp.int32, ((CONV_WIDTH - 1) * DN_SUB, span), 0)
    src_row = lax.broadcasted_iota(jnp.int32, ((CONV_WIDTH - 1) * DN_SUB, span), 1)
    back = (CONV_WIDTH - 1) - out_row // DN_SUB
    select = (src_row == DN_HALO + out_row % DN_SUB - back).astype(BF16)
    tiles = []
    for r in range(tb // DN_SUB):
        window = xp_ref[r * DN_SUB:r * DN_SUB + span, :]
        shifted = jnp.dot(select, window, preferred_element_type=F32)
        y = window[DN_HALO:, :].astype(F32) * convw[CONV_WIDTH - 1:CONV_WIDTH, :]
        for tap in range(CONV_WIDTH - 1):
            y = y + shifted[tap * DN_SUB:(tap + 1) * DN_SUB, :] * convw[tap:tap + 1, :]
        tiles.append(y)
    qkv = _silu(jnp.concatenate(tiles, axis=0))

    row = lax.broadcasted_iota(jnp.int32, (DN_SUB, DN_SUB), 0)
    col = lax.broadcasted_iota(jnp.int32, (DN_SUB, DN_SUB), 1)
    same_chunk = (row // CHUNK) == (col // CHUNK)
    lower_incl = same_chunk & (row >= col)
    strict_lower = same_chunk & (row > col)
    cum_lhs = jnp.concatenate([lower_incl.astype(F32), same_chunk.astype(F32)], axis=0)
    cum_rhs_t = (same_chunk & (row <= col)).astype(F32)

    ab = ab_ref[...]
    g_cols = -jnp.exp(prow_ref[0:1, :]) * _softplus(ab + prow_ref[1:2, :])
    g_rows = -jnp.exp(pcol_ref[:, 0:1]) * _softplus(abt_ref[...] + pcol_ref[:, 1:2])
    beta_cols = jax.nn.sigmoid(ab)

    nsub = tb // DN_SUB
    subs = [slice(s * DN_SUB, (s + 1) * DN_SUB) for s in range(nsub)]
    gl_rows, gcum_cols, gtot_cols, gcum_rows = [], [], [], []
    for rs in subs:
        cum = jnp.dot(cum_lhs, g_cols[rs, :], preferred_element_type=F32,
                      precision=lax.Precision.HIGHEST)
        gcum_cols.append(cum[:DN_SUB])
        gtot_cols.append(cum[DN_SUB:])
        gcum_rows.append(jnp.dot(g_rows[:, rs], cum_rhs_t, preferred_element_type=F32,
                                 precision=lax.Precision.HIGHEST))
        for c in range(DN_SUB // CHUNK):
            tot = jnp.exp(cum[DN_SUB + c * CHUNK:DN_SUB + c * CHUNK + 1, :])
            gl_rows.append(jnp.concatenate(
                [jnp.broadcast_to(tot[:, h:h + 1], (1, DN_HEAD_DIM)) for h in range(DN_HEADS)], axis=1))
    gl_ref[...] = jnp.concatenate(gl_rows, axis=0)

    chains = [(s, h) for s in range(nsub) for h in range(DN_HEADS)]

    def stack(fn):
        return jnp.stack([fn(s, h) for s, h in chains], axis=0)

    def head_cols(base):
        return stack(lambda s, h: qkv[subs[s], base + h * DN_HEAD_DIM:base + (h + 1) * DN_HEAD_DIM])

    q = head_cols(0)
    k = head_cols(DN_WIDTH)
    v = head_cols(2 * DN_WIDTH)
    gc = stack(lambda s, h: gcum_cols[s][:, h:h + 1])
    gt = stack(lambda s, h: gtot_cols[s][:, h:h + 1])
    gr = stack(lambda s, h: gcum_rows[s][h:h + 1, :])
    beta = stack(lambda s, h: beta_cols[subs[s], DN_HEADS + h:DN_HEADS + h + 1])

    q = q * lax.rsqrt(jnp.sum(q * q, axis=-1, keepdims=True) + 1e-6) * (DN_HEAD_DIM ** -0.5)
    k = k * lax.rsqrt(jnp.sum(k * k, axis=-1, keepdims=True) + 1e-6)
    lower3 = lower_incl[None]
    decay = jnp.where(lower3, jnp.exp(jnp.where(lower3, gc - gr, 0.0)), 0.0)
    k_beta = k * beta
    gram = _bmm_nt(jnp.concatenate([q, k_beta], axis=1).astype(BF16), k.astype(BF16))
    attn = jnp.where(lower3, gram[:, :DN_SUB] * decay, 0.0)
    lmat = jnp.where(strict_lower[None], gram[:, DN_SUB:] * decay, 0.0)
    tinv = _unit_lower_inverse(lmat, row, col)
    exp_gc = jnp.exp(gc)
    uw = _bmm(tinv.astype(BF16), jnp.concatenate([v * beta, k_beta * exp_gc], axis=2).astype(BF16))
    q_dec = (q * exp_gc).astype(BF16)
    k_end = k * jnp.exp(gt - gc)
    compact = attn[:, :, 0:CHUNK]
    for c in range(1, DN_SUB // CHUNK):
        compact = compact + attn[:, :, c * CHUNK:(c + 1) * CHUNK]
    compact = compact.astype(BF16)
    uw = uw.astype(BF16)

    for idx, (s, h) in enumerate(chains):
        rs = subs[s]
        hs = slice(h * DN_HEAD_DIM, (h + 1) * DN_HEAD_DIM)
        u_ref[rs, hs] = uw[idx, :, :DN_HEAD_DIM]
        w_ref[rs, hs] = uw[idx, :, DN_HEAD_DIM:]
        qd_ref[rs, hs] = q_dec[idx]
        ket_ref[0, hs, rs] = k_end[idx].T.astype(BF16)
        attn_ref[rs, h * CHUNK:(h + 1) * CHUNK] = compact[idx]


def _dn_scan_kernel(u_ref, w_ref, qd_ref, attn_ref, ket_ref, z_ref, gl_ref, normw_ref,
                    o_ref, state_ref):
    step = pl.program_id(0)

    @pl.when(step == 0)
    def _():
        state_ref[...] = jnp.zeros_like(state_ref)

    batch = u_ref.shape[0]
    normw = normw_ref[...]
    chains = [(b, h, slice(h * DN_HEAD_DIM, (h + 1) * DN_HEAD_DIM))
              for b in range(batch) for h in range(DN_HEADS)]
    states = [state_ref[b * DN_HEADS + h] for b, h, _ in chains]
    for c in range(DN_SCAN // CHUNK):
        rs = slice(c * CHUNK, (c + 1) * CHUNK)
        ws_qs = [jnp.dot(jnp.concatenate([w_ref[b, rs, hs], qd_ref[b, rs, hs]], axis=0),
                         state.astype(BF16), preferred_element_type=F32)
                 for (b, h, hs), state in zip(chains, states)]
        v_new = [(u_ref[b, rs, hs].astype(F32) - r[:CHUNK]).astype(BF16)
                 for (b, h, hs), r in zip(chains, ws_qs)]
        av_kv = [jnp.dot(jnp.concatenate([attn_ref[b, rs, h * CHUNK:(h + 1) * CHUNK],
                                          ket_ref[b, hs, rs]], axis=0),
                         vn, preferred_element_type=F32)
                 for (b, h, hs), vn in zip(chains, v_new)]
        states = [state * gl_ref[b, 0, c:c + 1, hs] + r[CHUNK:]
                  for (b, h, hs), state, r in zip(chains, states, av_kv)]
        for (b, h, hs), r1, r2 in zip(chains, ws_qs, av_kv):
            o = r1[CHUNK:] + r2[:CHUNK]
            o = o * lax.rsqrt(jnp.mean(o * o, axis=-1, keepdims=True) + RMS_EPS)
            o_ref[b, rs, hs] = (o * normw * _silu(z_ref[b, rs, hs].astype(F32))).astype(o_ref.dtype)
    for (b, h, _), state in zip(chains, states):
        state_ref[b * DN_HEADS + h] = state


def _deltanet(qkv, z, ab, abt, conv_w, a_log, dt_bias, dn_norm_w, batch, seq):
    t = batch * seq
    tb = min(DN_BLOCK, seq)
    blocks_per_seq = seq // tb
    prow = jnp.zeros((SUBLANES, LANES), F32).at[0, :DN_HEADS].set(a_log).at[1, :DN_HEADS].set(dt_bias)
    pcol = (jnp.zeros((SUBLANES, LANES), F32)
            .at[:DN_HEADS, 0].set(a_log).at[DN_HEADS:2 * DN_HEADS, 0].set(a_log)
            .at[:DN_HEADS, 1].set(dt_bias).at[DN_HEADS:2 * DN_HEADS, 1].set(dt_bias))

    def rows(width):
        return pl.BlockSpec((tb, width), lambda i: (i, 0))

    def const(shape):
        return pl.BlockSpec(shape, lambda i: (0,) * len(shape))

    u, w, qd, attn, ket, gl = pl.pallas_call(
        functools.partial(_dn_prep_kernel, blocks_per_seq=blocks_per_seq),
        grid=(t // tb,),
        in_specs=[rows(3 * DN_WIDTH),
                  pl.BlockSpec((DN_HALO, 3 * DN_WIDTH),
                               lambda i: (jnp.maximum(i * (tb // DN_HALO) - 1, 0), 0)),
                  rows(LANES), pl.BlockSpec((SUBLANES, tb), lambda i: (0, i)),
                  const((CONV_WIDTH, 3 * DN_WIDTH)), const((SUBLANES, LANES)), const((SUBLANES, LANES))],
        out_specs=[rows(DN_WIDTH), rows(DN_WIDTH), rows(DN_WIDTH), rows(DN_HEADS * CHUNK),
                   pl.BlockSpec((1, DN_WIDTH, tb), lambda i: (i // blocks_per_seq, 0, i % blocks_per_seq)),
                   pl.BlockSpec((tb // CHUNK, DN_WIDTH), lambda i: (i, 0))],
        out_shape=[jax.ShapeDtypeStruct((t, DN_WIDTH), BF16)] * 3
        + [jax.ShapeDtypeStruct((t, DN_HEADS * CHUNK), BF16),
           jax.ShapeDtypeStruct((batch, DN_WIDTH, seq), BF16),
           jax.ShapeDtypeStruct((t // CHUNK, DN_WIDTH), F32)],
        scratch_shapes=[pltpu.VMEM((DN_HALO + tb, 3 * DN_WIDTH), BF16)],
        compiler_params=_cparams(("parallel",)),
        name="deltanet_prep",
    )(qkv, qkv, ab, abt, conv_w, prow, pcol)

    def seq_rows(width):
        return pl.BlockSpec((batch, DN_SCAN, width), lambda c: (0, c, 0))

    def view(a):
        return a.reshape(batch, seq, a.shape[-1])

    o = pl.pallas_call(
        _dn_scan_kernel,
        grid=(seq // DN_SCAN,),
        in_specs=[seq_rows(DN_WIDTH), seq_rows(DN_WIDTH), seq_rows(DN_WIDTH), seq_rows(DN_HEADS * CHUNK),
                  pl.BlockSpec((batch, DN_WIDTH, DN_SCAN), lambda c: (0, 0, c)),
                  seq_rows(DN_WIDTH),
                  pl.BlockSpec((batch, 1, DN_SCAN // CHUNK, DN_WIDTH), lambda c: (0, c, 0, 0)),
                  pl.BlockSpec((1, DN_HEAD_DIM), lambda c: (0, 0))],
        out_specs=seq_rows(DN_WIDTH),
        out_shape=jax.ShapeDtypeStruct((batch, seq, DN_WIDTH), BF16),
        scratch_shapes=[pltpu.VMEM((batch * DN_HEADS, DN_HEAD_DIM, DN_HEAD_DIM), F32)],
        compiler_params=_cparams(("arbitrary",)),
        name="deltanet_scan",
    )(view(u), view(w), view(qd), view(attn), ket, view(z),
      gl.reshape(batch, seq // DN_SCAN, DN_SCAN // CHUNK, DN_WIDTH), dn_norm_w.reshape(1, DN_HEAD_DIM))
    return o.reshape(t, DN_WIDTH)


def _swa_kernel(q_ref, ktc_ref, ktp_ref, vc_ref, vp_ref, sink_ref, o_ref):
    qi = lax.broadcasted_iota(jnp.int32, (WINDOW, WINDOW), 0)
    kj = lax.broadcasted_iota(jnp.int32, (WINDOW, WINDOW), 1)
    from_prev = kj > qi
    no_prev = from_prev & (pl.program_id(1) == 0)
    kt = jnp.concatenate([ktp_ref[...], ktc_ref[...]], axis=1)
    v = jnp.concatenate([vp_ref[...], vc_ref[...]], axis=0)
    scores = []
    for j in range(SWA_BLOCKS):
        for hq in range(SW_Q_HEADS):
            hk = hq // SW_GROUP
            q = q_ref[j * WINDOW:(j + 1) * WINDOW, hq * SW_HEAD_DIM:(hq + 1) * SW_HEAD_DIM]
            s = jnp.dot(q, kt[hk * SW_HEAD_DIM:(hk + 1) * SW_HEAD_DIM, j * WINDOW:(j + 2) * WINDOW],
                        preferred_element_type=F32)
            s = jnp.where(from_prev, s[:, :WINDOW], s[:, WINDOW:])
            scores.append(jnp.where(no_prev, -jnp.inf, s) if j == 0 else s)
    s = jnp.concatenate(scores, axis=0)
    sink = sink_ref[...]
    m = jnp.maximum(jnp.max(s, axis=-1, keepdims=True), sink)
    p = jnp.exp(s - m).astype(BF16)
    sink_term = jnp.exp(sink - m)
    zero = jnp.zeros((WINDOW, WINDOW), BF16)
    ones = jnp.ones((WINDOW, SW_HEAD_DIM), BF16)
    v_ones = [[jnp.concatenate([v[jb * WINDOW:(jb + 1) * WINDOW, hk * SW_HEAD_DIM:(hk + 1) * SW_HEAD_DIM],
                                ones], axis=1) for hk in range(SW_KV_HEADS)]
              for jb in range(SWA_BLOCKS + 1)]
    pvs = []
    for j in range(SWA_BLOCKS):
        for hq in range(SW_Q_HEADS):
            hk = hq // SW_GROUP
            rs = slice((j * SW_Q_HEADS + hq) * WINDOW, (j * SW_Q_HEADS + hq + 1) * WINDOW)
            p_prev = jnp.where(from_prev, p[rs], zero)
            p_cur = jnp.where(from_prev, zero, p[rs])
            pvs.append(jnp.dot(p_prev, v_ones[j][hk], preferred_element_type=F32)
                       + jnp.dot(p_cur, v_ones[j + 1][hk], preferred_element_type=F32))
    pv = jnp.concatenate(pvs, axis=0)
    out = pv[:, :SW_HEAD_DIM] / (pv[:, SW_HEAD_DIM:SW_HEAD_DIM + 1] + sink_term)
    for j in range(SWA_BLOCKS):
        heads = [out[(j * SW_Q_HEADS + hq) * WINDOW:(j * SW_Q_HEADS + hq + 1) * WINDOW]
                 for hq in range(SW_Q_HEADS)]
        o_ref[j * WINDOW:(j + 1) * WINDOW, :] = jnp.concatenate(heads, axis=1).astype(o_ref.dtype)


def _swa(qsw, ktsw, vsw, sinks, batch, seq):
    t = batch * seq
    tq = SWA_BLOCKS * WINDOW
    nstep = seq // tq
    sink_rows = jnp.tile(jnp.repeat(sinks.astype(F32), WINDOW), SWA_BLOCKS).reshape(-1, 1)

    def cur(b, i):
        return b * nstep + i

    def prev(b, i):
        return (b * nstep + i) * SWA_BLOCKS - jnp.minimum(i, 1)

    return pl.pallas_call(
        _swa_kernel,
        grid=(batch, nstep),
        in_specs=[pl.BlockSpec((tq, SW_WIDTH), lambda b, i: (cur(b, i), 0)),
                  pl.BlockSpec((SW_KV_WIDTH, tq), lambda b, i: (0, cur(b, i))),
                  pl.BlockSpec((SW_KV_WIDTH, WINDOW), lambda b, i: (0, prev(b, i))),
                  pl.BlockSpec((tq, SW_KV_WIDTH), lambda b, i: (cur(b, i), 0)),
                  pl.BlockSpec((WINDOW, SW_KV_WIDTH), lambda b, i: (prev(b, i), 0)),
                  pl.BlockSpec((SWA_BLOCKS * SW_Q_HEADS * WINDOW, 1), lambda b, i: (0, 0))],
        out_specs=pl.BlockSpec((tq, SW_WIDTH), lambda b, i: (cur(b, i), 0)),
        out_shape=jax.ShapeDtypeStruct((t, SW_WIDTH), BF16),
        compiler_params=_cparams(("parallel", "parallel")),
        name="sliding_window_attn",
    )(qsw, ktsw, ktsw, vsw, vsw, sink_rows)


def _outproj_kernel(odn_ref, osw_ref, w_ref, x_ref, g_ref, b_ref, o_ref):
    mix = (jnp.dot(odn_ref[...], w_ref[0, 0:DN_WIDTH, :], preferred_element_type=F32)
           + jnp.dot(osw_ref[...], w_ref[0, DN_WIDTH:, :], preferred_element_type=F32))
    o_ref[...] = _layer_norm(ALPHA * x_ref[...] + mix, g_ref[...], b_ref[...])


def _outproj_ln(o_dn, o_sw, w_out_bf16, layer, x2d, g, b):
    t = x2d.shape[0]
    tm = min(TM_OUT, t)

    def rows(width):
        return pl.BlockSpec((tm, width), lambda i: (i, 0))

    vec = pl.BlockSpec((1, D_MODEL), lambda i: (0, 0))
    return pl.pallas_call(
        _outproj_kernel,
        grid=(t // tm,),
        in_specs=[rows(DN_WIDTH), rows(SW_WIDTH),
                  pl.BlockSpec((1, DN_WIDTH + SW_WIDTH, D_MODEL), lambda i: (layer, 0, 0)),
                  rows(D_MODEL), vec, vec],
        out_specs=rows(D_MODEL),
        out_shape=jax.ShapeDtypeStruct((t, D_MODEL), F32),
        compiler_params=_cparams(("parallel",)),
        name="out_proj_ln",
    )(o_dn, o_sw, w_out_bf16, x2d, g.reshape(1, D_MODEL), b.reshape(1, D_MODEL))


def _ffn_kernel(x_ref, wg_ref, wu_ref, wd_ref, g_ref, b_ref, o_ref, xb_ref, acc_ref):
    f = pl.program_id(1)

    @pl.when(f == 0)
    def _():
        xb_ref[...] = x_ref[...].astype(BF16)
        acc_ref[...] = jnp.zeros_like(acc_ref)

    xb = xb_ref[...]
    gate = jnp.dot(xb, wg_ref[0].astype(BF16), preferred_element_type=F32)
    up = jnp.dot(xb, wu_ref[0].astype(BF16), preferred_element_type=F32)
    hid = (_silu(gate) * up).astype(BF16)
    acc_ref[...] += jnp.dot(hid, wd_ref[0].astype(BF16), preferred_element_type=F32)

    @pl.when(f == pl.num_programs(1) - 1)
    def _():
        o_ref[...] = _layer_norm(ALPHA * x_ref[...] + acc_ref[...], g_ref[...], b_ref[...])


def _ffn_ln(x2d, w_gate, w_up, w_down, idx, g, b):
    t = x2d.shape[0]
    tm = min(TM_FFN, t)
    vec = pl.BlockSpec((1, D_MODEL), lambda i, f: (0, 0))
    return pl.pallas_call(
        _ffn_kernel,
        grid=(t // tm, D_FF // TF_FFN),
        in_specs=[pl.BlockSpec((tm, D_MODEL), lambda i, f: (i, 0)),
                  pl.BlockSpec((1, D_MODEL, TF_FFN), lambda i, f: (idx, 0, f)),
                  pl.BlockSpec((1, D_MODEL, TF_FFN), lambda i, f: (idx, 0, f)),
                  pl.BlockSpec((1, TF_FFN, D_MODEL), lambda i, f: (idx, f, 0)),
                  vec, vec],
        out_specs=pl.BlockSpec((tm, D_MODEL), lambda i, f: (i, 0)),
        out_shape=jax.ShapeDtypeStruct((t, D_MODEL), F32),
        scratch_shapes=[pltpu.VMEM((tm, D_MODEL), BF16), pltpu.VMEM((tm, D_MODEL), F32)],
        compiler_params=_cparams(("parallel", "arbitrary")),
        name="dense_swiglu_ln",
    )(x2d, w_gate, w_up, w_down, g.reshape(1, D_MODEL), b.reshape(1, D_MODEL))


R_E1, R_E2, R_RANK1, R_RANK2, R_G1, R_G2 = range(6)


def _router_kernel(x_ref, wr_ref, meta_ref, metat_ref, count_ref, carry_ref):
    i = pl.program_id(0)

    @pl.when(i == 0)
    def _():
        carry_ref[...] = jnp.zeros_like(carry_ref)

    tm = x_ref.shape[0]
    logits = jnp.dot(x_ref[...], wr_ref[...], preferred_element_type=F32,
                     precision=lax.Precision.HIGHEST)
    lane = lax.broadcasted_iota(jnp.int32, logits.shape, 1)
    logits = jnp.where(lane < N_EXPERTS, logits, -jnp.inf)
    m1 = jnp.max(logits, axis=-1, keepdims=True)
    e1 = jnp.min(jnp.where(logits == m1, lane, LANES), axis=-1, keepdims=True)
    rest = jnp.where(lane == e1, -jnp.inf, logits)
    m2 = jnp.max(rest, axis=-1, keepdims=True)
    e2 = jnp.min(jnp.where(rest == m2, lane, LANES), axis=-1, keepdims=True)
    ex = jnp.exp(m2 - m1)
    g1 = 1.0 / (1.0 + ex)
    g2 = ex / (1.0 + ex)

    hit = ((lane == e1) | (lane == e2)).astype(BF16)
    r = lax.broadcasted_iota(jnp.int32, (tm, tm), 0)
    c = lax.broadcasted_iota(jnp.int32, (tm, tm), 1)
    before = jnp.dot((c < r).astype(BF16), hit, preferred_element_type=F32) + carry_ref[0:1, :]
    rank1 = jnp.sum(jnp.where(lane == e1, before, 0.0), axis=-1, keepdims=True)
    rank2 = jnp.sum(jnp.where(lane == e2, before, 0.0), axis=-1, keepdims=True)
    carry_ref[0:1, :] = carry_ref[0:1, :] + jnp.sum(hit.astype(F32), axis=0, keepdims=True)
    count_ref[...] = jnp.broadcast_to(carry_ref[0:1, :], count_ref.shape)

    meta = jnp.zeros((tm, LANES), F32)
    for idx, val in ((R_E1, e1.astype(F32)), (R_E2, e2.astype(F32)), (R_RANK1, rank1),
                     (R_RANK2, rank2), (R_G1, g1), (R_G2, g2)):
        meta = jnp.where(lane == idx, val, meta)
    meta_ref[...] = meta[:, :SUBLANES]
    metat_ref[...] = meta.T[:SUBLANES, :]


def _router(x2d, router_w):
    t = x2d.shape[0]
    tm = min(TM_ROUTE, t)
    wr = jnp.zeros((D_MODEL, LANES), F32).at[:, :N_EXPERTS].set(router_w)
    return pl.pallas_call(
        _router_kernel,
        grid=(t // tm,),
        in_specs=[pl.BlockSpec((tm, D_MODEL), lambda i: (i, 0)),
                  pl.BlockSpec((D_MODEL, LANES), lambda i: (0, 0))],
        out_specs=[pl.BlockSpec((tm, SUBLANES), lambda i: (i, 0)),
                   pl.BlockSpec((SUBLANES, tm), lambda i: (0, i)),
                   pl.BlockSpec((SUBLANES, LANES), lambda i: (0, 0))],
        out_shape=[jax.ShapeDtypeStruct((t, SUBLANES), F32),
                   jax.ShapeDtypeStruct((SUBLANES, t), F32),
                   jax.ShapeDtypeStruct((SUBLANES, LANES), F32)],
        scratch_shapes=[pltpu.VMEM((SUBLANES, LANES), F32)],
        compiler_params=_cparams(("arbitrary",)),
        name="moe_router",
    )(x2d, wr)


def _row_copy(src_ref, src_row, dst_ref, dst_row, sem):
    return pltpu.make_async_copy(src_ref.at[pl.ds(src_row, 1), :], dst_ref.at[pl.ds(dst_row, 1), :], sem)


def _dispatch_kernel(dest1_ref, dest2_ref, x_ref, xs_ref, sem):
    base = pl.program_id(0) * TM_MOVE

    def issue(j, carry):
        tok = base + j
        _row_copy(x_ref, j, xs_ref, dest1_ref[tok], sem).start(priority=0)
        _row_copy(x_ref, j, xs_ref, dest2_ref[tok], sem).start(priority=1)
        return carry

    lax.fori_loop(0, TM_MOVE, issue, 0, unroll=MOVE_UNROLL)
    for _ in range(2):
        pltpu.make_async_copy(x_ref, xs_ref.at[pl.ds(0, TM_MOVE), :], sem).wait()


def _dispatch(x2d, dest1, dest2, rows_sorted):
    t = x2d.shape[0]
    return pl.pallas_call(
        _dispatch_kernel,
        grid_spec=pltpu.PrefetchScalarGridSpec(
            num_scalar_prefetch=2,
            grid=(t // TM_MOVE,),
            in_specs=[pl.BlockSpec((TM_MOVE, D_MODEL), lambda i, d1, d2: (i, 0))],
            out_specs=pl.BlockSpec(memory_space=pl.ANY),
            scratch_shapes=[pltpu.SemaphoreType.DMA(())]),
        out_shape=jax.ShapeDtypeStruct((rows_sorted, D_MODEL), F32),
        compiler_params=_cparams(("arbitrary",)),
        name="moe_dispatch",
    )(dest1, dest2, x2d)


def _expert_kernel(vtile_ref, vexpert_ref, vfidx_ref, vlo_ref, vhi_ref, vfirst_ref, vlast_ref,
                   xs_ref, wg_ref, wu_ref, wd_ref, ys_ref, xb_ref, acc_ref, wgb_ref, wub_ref, wdb_ref):
    v = pl.program_id(0)
    f = pl.program_id(1)
    lo = vlo_ref[v]
    hi = vhi_ref[v]
    whole = (lo == 0) & (hi == TM_EXPERT)

    def swiglu_rows(rows, wg, wu, wd):
        xb = xb_ref[rows, :]
        gate = jnp.dot(xb, wg, preferred_element_type=F32)
        up = jnp.dot(xb, wu, preferred_element_type=F32)
        hid = (_silu(gate) * up).astype(BF16)
        acc_ref[rows, :] += jnp.dot(hid, wd, preferred_element_type=F32)

    @pl.when(hi > lo)
    def _():
        @pl.when(f == 0)
        def _():
            rowid = lax.broadcasted_iota(jnp.int32, xs_ref.shape, 0)
            xb_ref[...] = jnp.where((rowid >= lo) & (rowid < hi), xs_ref[...], 0.0).astype(BF16)

        @pl.when((f == 0) & (vfirst_ref[v] == 1))
        def _():
            acc_ref[...] = jnp.zeros_like(acc_ref)

        @pl.when(whole)
        def _():
            swiglu_rows(slice(None), wg_ref[0, 0].astype(BF16), wu_ref[0, 0].astype(BF16),
                        wd_ref[0, 0].astype(BF16))

        @pl.when(jnp.logical_not(whole))
        def _():
            wgb_ref[...] = wg_ref[0, 0].astype(BF16)
            wub_ref[...] = wu_ref[0, 0].astype(BF16)
            wdb_ref[...] = wd_ref[0, 0].astype(BF16)
            for sb in range(TM_EXPERT // TM_EXPERT_SUB):
                @pl.when((sb * TM_EXPERT_SUB < hi) & ((sb + 1) * TM_EXPERT_SUB > lo))
                def _():
                    swiglu_rows(slice(sb * TM_EXPERT_SUB, (sb + 1) * TM_EXPERT_SUB),
                                wgb_ref[...], wub_ref[...], wdb_ref[...])

        @pl.when((f == pl.num_programs(1) - 1) & (vlast_ref[v] == 1))
        def _():
            ys_ref[...] = acc_ref[...]


def _expert_visits(counts, n_tiles):
    n_visits = n_tiles + N_EXPERTS - 1
    ends = jnp.cumsum(counts)
    offs = ends - counts
    first_tile = offs // TM_EXPERT
    per_expert = jnp.where(counts > 0, (ends - 1) // TM_EXPERT - first_tile + 1, 0)
    vend = jnp.cumsum(per_expert)
    vstart = vend - per_expert
    total = vend[-1]
    vid = jnp.arange(n_visits, dtype=jnp.int32)
    real = vid < total
    vid_c = jnp.minimum(vid, jnp.maximum(total - 1, 0))
    vexpert = jnp.minimum(jnp.sum(vid_c[:, None] >= vend[None, :], axis=1), N_EXPERTS - 1)
    vtile = first_tile[vexpert] + vid_c - vstart[vexpert]
    lo = jnp.maximum(offs[vexpert], vtile * TM_EXPERT) - vtile * TM_EXPERT
    hi = jnp.minimum(ends[vexpert], (vtile + 1) * TM_EXPERT) - vtile * TM_EXPERT
    prev_tile = jnp.concatenate([jnp.full((1,), -1, vtile.dtype), vtile[:-1]])
    next_tile = jnp.concatenate([vtile[1:], jnp.full((1,), -1, vtile.dtype)])
    first = real & (vtile != prev_tile)
    last = real & ((vtile != next_tile) | (vid == total - 1))
    i32 = lambda a: a.astype(jnp.int32)
    fpin = jnp.where(real, -1, D_FF // TF_FFN - 1)
    return (i32(vtile), i32(vexpert), i32(fpin), i32(jnp.where(real, lo, 0)),
            i32(jnp.where(real, hi, 0)), i32(first), i32(last), i32(offs))


def _expert_ffn(xs, visits, w_gate, w_up, w_down, idx):
    rows_sorted = xs.shape[0]
    n_visits = visits[0].shape[0]
    nf = D_FF // TF_FFN

    def f_idx(v, f, vf):
        return jnp.where(vf[v] >= 0, vf[v], f)

    def rows_map(v, f, vt, ve, vf, *_):
        return (vt[v], 0)

    def wcol_map(v, f, vt, ve, vf, *_):
        return (idx, ve[v], 0, f_idx(v, f, vf))

    def wrow_map(v, f, vt, ve, vf, *_):
        return (idx, ve[v], f_idx(v, f, vf), 0)

    return pl.pallas_call(
        _expert_kernel,
        grid_spec=pltpu.PrefetchScalarGridSpec(
            num_scalar_prefetch=len(visits),
            grid=(n_visits, nf),
            in_specs=[pl.BlockSpec((TM_EXPERT, D_MODEL), rows_map),
                      pl.BlockSpec((1, 1, D_MODEL, TF_FFN), wcol_map),
                      pl.BlockSpec((1, 1, D_MODEL, TF_FFN), wcol_map),
                      pl.BlockSpec((1, 1, TF_FFN, D_MODEL), wrow_map)],
            out_specs=pl.BlockSpec((TM_EXPERT, D_MODEL), rows_map),
            scratch_shapes=[pltpu.VMEM((TM_EXPERT, D_MODEL), BF16),
                            pltpu.VMEM((TM_EXPERT, D_MODEL), F32),
                            pltpu.VMEM((D_MODEL, TF_FFN), BF16),
                            pltpu.VMEM((D_MODEL, TF_FFN), BF16),
                            pltpu.VMEM((TF_FFN, D_MODEL), BF16)]),
        out_shape=jax.ShapeDtypeStruct((rows_sorted, D_MODEL), F32),
        compiler_params=_cparams(("arbitrary", "arbitrary")),
        name="moe_expert_swiglu",
    )(*visits, xs, w_gate, w_up, w_down)


def _combine_kernel(dest1_ref, dest2_ref, ys_ref, x_ref, meta_ref, g_ref, b_ref, o_ref,
                    buf1_ref, buf2_ref, sem):
    base = pl.program_id(0) * TM_MOVE

    def issue(j, carry):
        tok = base + j
        _row_copy(ys_ref, dest1_ref[tok], buf1_ref, j, sem).start(priority=0)
        _row_copy(ys_ref, dest2_ref[tok], buf2_ref, j, sem).start(priority=1)
        return carry

    lax.fori_loop(0, TM_MOVE, issue, 0, unroll=MOVE_UNROLL)
    for buf_ref in (buf1_ref, buf2_ref):
        pltpu.make_async_copy(ys_ref.at[pl.ds(0, TM_MOVE), :], buf_ref, sem).wait()

    meta = meta_ref[...]
    g1 = meta[:, R_G1:R_G1 + 1]
    g2 = meta[:, R_G2:R_G2 + 1]
    moe = g1 * buf1_ref[...] + g2 * buf2_ref[...]
    o_ref[...] = _layer_norm(ALPHA * x_ref[...] + moe, g_ref[...], b_ref[...])


def _combine_ln(ys, dest1, dest2, x2d, meta, g, b):
    t = x2d.shape[0]
    vec = pl.BlockSpec((1, D_MODEL), lambda i, d1, d2: (0, 0))
    return pl.pallas_call(
        _combine_kernel,
        grid_spec=pltpu.PrefetchScalarGridSpec(
            num_scalar_prefetch=2,
            grid=(t // TM_MOVE,),
            in_specs=[pl.BlockSpec(memory_space=pl.ANY),
                      pl.BlockSpec((TM_MOVE, D_MODEL), lambda i, d1, d2: (i, 0)),
                      pl.BlockSpec((TM_MOVE, SUBLANES), lambda i, d1, d2: (i, 0)),
                      vec, vec],
            out_specs=pl.BlockSpec((TM_MOVE, D_MODEL), lambda i, d1, d2: (i, 0)),
            scratch_shapes=[pltpu.VMEM((TM_MOVE, D_MODEL), F32),
                            pltpu.VMEM((TM_MOVE, D_MODEL), F32),
                            pltpu.SemaphoreType.DMA(())]),
        out_shape=jax.ShapeDtypeStruct((t, D_MODEL), F32),
        compiler_params=_cparams(("arbitrary",)),
        name="moe_combine_ln",
    )(dest1, dest2, ys, x2d, meta, g.reshape(1, D_MODEL), b.reshape(1, D_MODEL))


def _moe_ln(x2d, router_w, w_gate, w_up, w_down, idx, g, b):
    t = x2d.shape[0]
    meta, meta_t, counts = _router(x2d, router_w)
    counts = counts[0, :N_EXPERTS].astype(jnp.int32)
    *visits, group_start = _expert_visits(counts, (2 * t) // TM_EXPERT)

    def slot(expert_row, rank_row):
        expert = meta_t[expert_row].astype(jnp.int32)
        start = sum(jnp.where(expert == e, group_start[e], 0) for e in range(N_EXPERTS))
        return start + meta_t[rank_row].astype(jnp.int32)

    dest1 = slot(R_E1, R_RANK1)
    dest2 = slot(R_E2, R_RANK2)
    xs = _dispatch(x2d, dest1, dest2, 2 * t)
    ys = _expert_ffn(xs, tuple(visits), w_gate, w_up, w_down, idx)
    return _combine_ln(ys, dest1, dest2, x2d, meta, g, b)


def _pack_w_in(w):
    dn = 4 * DN_WIDTH
    ab = w[..., dn:dn + 2 * DN_HEADS]
    pad = jnp.zeros(w.shape[:-1] + (LANES - 2 * DN_HEADS,), w.dtype)
    packed = jnp.concatenate([w[..., :dn], w[..., dn + 2 * DN_HEADS:], ab, pad], axis=-1).astype(BF16)
    return packed, jnp.swapaxes(ab, -1, -2).astype(BF16)


@jax.jit
def _trunk(x, positions, w_in, conv_w, a_log, dt_bias, dn_norm_w, sinks, w_out, ln_g, ln_b,
           ffn_w_gate, ffn_w_up, ffn_w_down, router_w, moe_w_gate, moe_w_up, moe_w_down):
    batch, seq, _ = x.shape
    x2d = x.reshape(batch * seq, D_MODEL)
    cos, sin = _rope_tables(positions)
    w_packed, wabt = _pack_w_in(w_in)
    w_out_bf16 = w_out.astype(BF16)
    for layer in range(DEPTH):
        qkv, z, qsw, ktsw, vsw, ab, abt = _inproj(x2d, w_packed, wabt, layer, cos, sin)
        o_dn = _deltanet(qkv, z, ab, abt, conv_w[layer], a_log[layer], dt_bias[layer],
                         dn_norm_w[layer], batch, seq)
        o_sw = _swa(qsw, ktsw, vsw, sinks[layer], batch, seq)
        x2d = _outproj_ln(o_dn, o_sw, w_out_bf16, layer, x2d, ln_g[layer, 0], ln_b[layer, 0])
        i = layer // 2
        if layer % 2 == 0:
            x2d = _ffn_ln(x2d, ffn_w_gate, ffn_w_up, ffn_w_down, i, ln_g[layer, 1], ln_b[layer, 1])
        else:
            x2d = _moe_ln(x2d, router_w[i], moe_w_gate, moe_w_up, moe_w_down, i,
                          ln_g[layer, 1], ln_b[layer, 1])
    return x2d.reshape(batch, seq, D_MODEL)


def kernel(x, positions, w_in, conv_w, a_log, dt_bias, dn_norm_w, sinks, w_out, ln_g, ln_b,
           ffn_w_gate, ffn_w_up, ffn_w_down, router_w, moe_w_gate, moe_w_up, moe_w_down):
    return _trunk(x, positions, w_in, conv_w, a_log, dt_bias, dn_norm_w, sinks, w_out, ln_g, ln_b,
                  ffn_w_gate, ffn_w_up, ffn_w_down, router_w, moe_w_gate, moe_w_up, moe_w_down)
```

```python
import functools

import jax
import jax.numpy as jnp
import numpy as np
from jax import lax
from jax.experimental import pallas as pl
from jax.experimental.pallas import tpu as pltpu

F32 = jnp.float32
BF16 = jnp.bfloat16

D_MODEL = 1024
DEPTH = 4
DN_HEADS = 4
DN_HEAD_DIM = 128
DN_WIDTH = DN_HEADS * DN_HEAD_DIM
CONV_WIDTH = 4
CHUNK = 64
SW_Q_HEADS = 8
SW_KV_HEADS = 2
SW_GROUP = SW_Q_HEADS // SW_KV_HEADS
SW_HEAD_DIM = 64
SW_WIDTH = SW_Q_HEADS * SW_HEAD_DIM
SW_KV_WIDTH = SW_KV_HEADS * SW_HEAD_DIM
WINDOW = 128
ROPE_THETA = 10000.0
D_FF = 3584
N_EXPERTS = 8
ALPHA = (2.0 * DEPTH) ** 0.25
LN_EPS = 1e-5
RMS_EPS = 1e-6

LANES = 128
SUBLANES = 8
DN_HALO = 16
DN_BLOCK = 512
DN_SUB = 128
DN_SCAN = 128
SWA_BLOCKS = 4
VMEM_LIMIT = 56 * 1024 * 1024

C_QKV = (0, 3 * DN_WIDTH)
C_Z = (C_QKV[1], C_QKV[1] + DN_WIDTH)
C_QSW = (C_Z[1], C_Z[1] + SW_WIDTH)
C_KVSW = (C_QSW[1], C_QSW[1] + 2 * SW_KV_WIDTH)
C_AB = (C_KVSW[1], C_KVSW[1] + LANES)
IN_COLS_PACKED = C_AB[1]

TM_PROJ = 512
TM_ROPE = 2048
TM_FFN = 1024
TF_FFN = 512
TM_ROUTE = 1024
TM_EXPERT = 1024
TM_EXPERT_SUB = 256
TM_OUT = 1024
TM_MOVE = 512
TM_GATHER = 256
COMBINE_SUBS = 8
COMBINE_SLOTS = 3


def _cparams(sem):
    return pltpu.CompilerParams(dimension_semantics=sem, vmem_limit_bytes=VMEM_LIMIT)


def _bdot(a, b):
    return jnp.dot(a.astype(BF16), b.astype(BF16), preferred_element_type=F32)


def _bdot_nt(a, b):
    return lax.dot_general(a.astype(BF16), b.astype(BF16), (((1,), (1,)), ((), ())),
                           preferred_element_type=F32)


def _silu(x):
    return x * jax.nn.sigmoid(x)


def _softplus(x):
    return jnp.maximum(x, 0.0) + jnp.log1p(jnp.exp(-jnp.abs(x)))


def _layer_norm(y, g, b):
    mu = jnp.mean(y, axis=-1, keepdims=True)
    d = y - mu
    var = jnp.mean(d * d, axis=-1, keepdims=True)
    return d * lax.rsqrt(var + LN_EPS) * g + b


def _rope_table_kernel(pos_ref, invf_ref, sign_ref, cos_ref, sin_ref):
    ang = pos_ref[...].astype(F32) * invf_ref[...]
    cos_ref[...] = jnp.cos(ang)
    sin_ref[...] = jnp.sin(ang) * sign_ref[...]


def _rope_tables(positions):
    t = positions.size
    half = SW_HEAD_DIM // 2
    inv_freq = ROPE_THETA ** (-jnp.arange(0, SW_HEAD_DIM, 2, dtype=F32) / SW_HEAD_DIM)
    reps = LANES // half
    invf = jnp.tile(inv_freq, reps).reshape(1, LANES)
    sign = jnp.tile(jnp.concatenate([-jnp.ones((half,), F32), jnp.ones((half,), F32)]),
                    LANES // SW_HEAD_DIM).reshape(1, LANES)
    tm = min(TM_ROPE, t)
    row = pl.BlockSpec((1, LANES), lambda i: (0, 0))
    tab = pl.BlockSpec((tm, LANES), lambda i: (i, 0))
    return pl.pallas_call(
        _rope_table_kernel,
        grid=(t // tm,),
        in_specs=[pl.BlockSpec((tm, 1), lambda i: (i, 0)), row, row],
        out_specs=[tab, tab],
        out_shape=[jax.ShapeDtypeStruct((t, LANES), F32)] * 2,
        compiler_params=_cparams(("parallel",)),
        name="rope_tables",
    )(positions.reshape(t, 1), invf, sign)


def _rope(x, cos, sin_signed):
    width = x.shape[-1]
    half = SW_HEAD_DIM // 2
    lane = lax.broadcasted_iota(jnp.int32, x.shape, 1)
    first = (lane % SW_HEAD_DIM) < half
    partner = jnp.where(first, pltpu.roll(x, width - half, 1), pltpu.roll(x, half, 1))
    return x * cos + partner * sin_signed


def _inproj_kernel(x_ref, w_ref, wabt_ref, cos_ref, sin_ref,
                   qkv_ref, z_ref, qsw_ref, ktsw_ref, vsw_ref, ab_ref, abt_ref):
    xb = x_ref[...].astype(BF16)

    def mm(cols):
        return jnp.dot(xb, w_ref[0, :, cols[0]:cols[1]], preferred_element_type=F32)

    qkv_ref[...] = mm(C_QKV).astype(BF16)
    z_ref[...] = mm(C_Z).astype(BF16)
    cos = cos_ref[...]
    sin = sin_ref[...]
    reps = SW_WIDTH // LANES
    q = _rope(mm(C_QSW), jnp.tile(cos, (1, reps)), jnp.tile(sin, (1, reps))) * (SW_HEAD_DIM ** -0.5)
    qsw_ref[...] = q.astype(BF16)
    kv = mm(C_KVSW)
    ktsw_ref[...] = _rope(kv[:, :SW_KV_WIDTH], cos, sin).T.astype(BF16)
    vsw_ref[...] = kv[:, SW_KV_WIDTH:].astype(BF16)
    ab_ref[...] = mm(C_AB)
    abt_ref[...] = lax.dot_general(wabt_ref[0], xb, (((1,), (1,)), ((), ())),
                                   preferred_element_type=F32)


def _inproj(x2d, w_packed, wabt, layer, cos, sin):
    t = x2d.shape[0]
    tm = min(TM_PROJ, t)

    def rows(width):
        return pl.BlockSpec((tm, width), lambda i: (i, 0))

    return pl.pallas_call(
        _inproj_kernel,
        grid=(t // tm,),
        in_specs=[rows(D_MODEL),
                  pl.BlockSpec((1, D_MODEL, IN_COLS_PACKED), lambda i: (layer, 0, 0)),
                  pl.BlockSpec((1, SUBLANES, D_MODEL), lambda i: (layer, 0, 0)),
                  rows(LANES), rows(LANES)],
        out_specs=[rows(3 * DN_WIDTH), rows(DN_WIDTH), rows(SW_WIDTH),
                   pl.BlockSpec((SW_KV_WIDTH, tm), lambda i: (0, i)), rows(SW_KV_WIDTH),
                   rows(LANES), pl.BlockSpec((SUBLANES, tm), lambda i: (0, i))],
        out_shape=[jax.ShapeDtypeStruct((t, 3 * DN_WIDTH), BF16),
                   jax.ShapeDtypeStruct((t, DN_WIDTH), BF16),
                   jax.ShapeDtypeStruct((t, SW_WIDTH), BF16),
                   jax.ShapeDtypeStruct((SW_KV_WIDTH, t), BF16),
                   jax.ShapeDtypeStruct((t, SW_KV_WIDTH), BF16),
                   jax.ShapeDtypeStruct((t, LANES), F32),
                   jax.ShapeDtypeStruct((SUBLANES, t), F32)],
        compiler_params=_cparams(("parallel",)),
        name="in_proj",
    )(x2d, w_packed, wabt, cos, sin)


def _bmm(a, b):
    return lax.dot_general(a, b, (((2,), (1,)), ((0,), (0,))), preferred_element_type=F32)


def _bmm_nt(a, b):
    return lax.dot_general(a, b, (((2,), (2,)), ((0,), (0,))), preferred_element_type=F32)


def _unit_lower_inverse(lmat, row, col):
    same16 = ((row // 16) == (col // 16))[None]
    same32 = ((row // 32) == (col // 32))[None]
    eye = (row == col).astype(F32)[None]
    ld = jnp.where(same16, lmat, 0.0)
    inv = eye - ld
    power = ld.astype(BF16)
    for _ in range(3):
        power = _bmm(power, power).astype(BF16)
        inv = inv + _bmm(inv.astype(BF16), power)
    lmat_bf16 = lmat.astype(BF16)
    for off_diag in (same32 & ~same16, ~same32):
        c = jnp.where(off_diag, lmat_bf16, jnp.zeros_like(lmat_bf16))
        inv_bf16 = inv.astype(BF16)
        inv = inv - _bmm(_bmm(inv_bf16, c).astype(BF16), inv_bf16)
    return inv


def _dn_prep_kernel(qkv_ref, halo_ref, ab_ref, abt_ref, convw_ref, prow_ref, pcol_ref,
                    u_ref, w_ref, qd_ref, attn_ref, ket_ref, gl_ref, xp_ref, *, blocks_per_seq):
    tb = qkv_ref.shape[0]
    first = (pl.program_id(0) % blocks_per_seq) == 0

    halo = halo_ref[...]
    xp_ref[0:DN_HALO, :] = jnp.where(first, jnp.zeros_like(halo), halo)
    xp_ref[DN_HALO:DN_HALO + tb, :] = qkv_ref[...]
    convw = convw_ref[...]
    span = DN_HALO + DN_SUB
    out_row = lax.broadcasted_iota(jnp.int32, ((CONV_WIDTH - 1) * DN_SUB, span), 0)
    src_row = lax.broadcasted_iota(jnp.int32, ((CONV_WIDTH - 1) * DN_SUB, span), 1)
    back = (CONV_WIDTH - 1) - out_row // DN_SUB
    select = (src_row == DN_HALO + out_row % DN_SUB - back).astype(BF16)
    tiles = []
    for r in range(tb // DN_SUB):
        window = xp_ref[r * DN_SUB:r * DN_SUB + span, :]
        shifted = jnp.dot(select, window, preferred_element_type=F32)
        y = window[DN_HALO:, :].astype(F32) * convw[CONV_WIDTH - 1:CONV_WIDTH, :]
        for tap in range(CONV_WIDTH - 1):
            y = y + shifted[tap * DN_SUB:(tap + 1) * DN_SUB, :] * convw[tap:tap + 1, :]
        tiles.append(y)
    qkv = _silu(jnp.concatenate(tiles, axis=0))

    row = lax.broadcasted_iota(jnp.int32, (DN_SUB, DN_SUB), 0)
    col = lax.broadcasted_iota(jnp.int32, (DN_SUB, DN_SUB), 1)
    same_chunk = (row // CHUNK) == (col // CHUNK)
    lower_incl = same_chunk & (row >= col)
    strict_lower = same_chunk & (row > col)
    cum_lhs = jnp.concatenate([lower_incl.astype(F32), same_chunk.astype(F32)], axis=0)
    cum_rhs_t = (same_chunk & (row <= col)).astype(F32)

    ab = ab_ref[...]
    g_cols = -jnp.exp(prow_ref[0:1, :]) * _softplus(ab + prow_ref[1:2, :])
    g_rows = -jnp.exp(pcol_ref[:, 0:1]) * _softplus(abt_ref[...] + pcol_ref[:, 1:2])
    beta_cols = jax.nn.sigmoid(ab)

    nsub = tb // DN_SUB
    subs = [slice(s * DN_SUB, (s + 1) * DN_SUB) for s in range(nsub)]
    gl_rows, gcum_cols, gtot_cols, gcum_rows = [], [], [], []
    for rs in subs:
        cum = jnp.dot(cum_lhs, g_cols[rs, :], preferred_element_type=F32,
                      precision=lax.Precision.HIGHEST)
        gcum_cols.append(cum[:DN_SUB])
        gtot_cols.append(cum[DN_SUB:])
        gcum_rows.append(jnp.dot(g_rows[:, rs], cum_rhs_t, preferred_element_type=F32,
                                 precision=lax.Precision.HIGHEST))
        for c in range(DN_SUB // CHUNK):
            tot = jnp.exp(cum[DN_SUB + c * CHUNK:DN_SUB + c * CHUNK + 1, :])
            gl_rows.append(jnp.concatenate(
                [jnp.broadcast_to(tot[:, h:h + 1], (1, DN_HEAD_DIM)) for h in range(DN_HEADS)], axis=1))
    gl_ref[...] = jnp.concatenate(gl_rows, axis=0)

    chains = [(s, h) for s in range(nsub) for h in range(DN_HEADS)]

    def stack(fn):
        return jnp.stack([fn(s, h) for s, h in chains], axis=0)

    def head_cols(base):
        return stack(lambda s, h: qkv[subs[s], base + h * DN_HEAD_DIM:base + (h + 1) * DN_HEAD_DIM])

    q = head_cols(0)
    k = head_cols(DN_WIDTH)
    v = head_cols(2 * DN_WIDTH)
    gc = stack(lambda s, h: gcum_cols[s][:, h:h + 1])
    gt = stack(lambda s, h: gtot_cols[s][:, h:h + 1])
    gr = stack(lambda s, h: gcum_rows[s][h:h + 1, :])
    beta = stack(lambda s, h: beta_cols[subs[s], DN_HEADS + h:DN_HEADS + h + 1])

    q = q * lax.rsqrt(jnp.sum(q * q, axis=-1, keepdims=True) + 1e-6) * (DN_HEAD_DIM ** -0.5)
    k = k * lax.rsqrt(jnp.sum(k * k, axis=-1, keepdims=True) + 1e-6)
    lower3 = lower_incl[None]
    decay = jnp.where(lower3, jnp.exp(jnp.where(lower3, gc - gr, 0.0)), 0.0)
    k_beta = k * beta
    gram = _bmm_nt(jnp.concatenate([q, k_beta], axis=1).astype(BF16), k.astype(BF16))
    attn = jnp.where(lower3, gram[:, :DN_SUB] * decay, 0.0)
    lmat = jnp.where(strict_lower[None], gram[:, DN_SUB:] * decay, 0.0)
    tinv = _unit_lower_inverse(lmat, row, col)
    exp_gc = jnp.exp(gc)
    uw = _bmm(tinv.astype(BF16), jnp.concatenate([v * beta, k_beta * exp_gc], axis=2).astype(BF16))
    q_dec = (q * exp_gc).astype(BF16)
    k_end = k * jnp.exp(gt - gc)
    compact = attn[:, :, 0:CHUNK]
    for c in range(1, DN_SUB // CHUNK):
        compact = compact + attn[:, :, c * CHUNK:(c + 1) * CHUNK]
    compact = compact.astype(BF16)
    uw = uw.astype(BF16)

    for idx, (s, h) in enumerate(chains):
        rs = subs[s]
        hs = slice(h * DN_HEAD_DIM, (h + 1) * DN_HEAD_DIM)
        u_ref[rs, hs] = uw[idx, :, :DN_HEAD_DIM]
        w_ref[rs, hs] = uw[idx, :, DN_HEAD_DIM:]
        qd_ref[rs, hs] = q_dec[idx]
        ket_ref[0, hs, rs] = k_end[idx].T.astype(BF16)
        attn_ref[rs, h * CHUNK:(h + 1) * CHUNK] = compact[idx]


def _dn_scan_kernel(u_ref, w_ref, qd_ref, attn_ref, ket_ref, z_ref, gl_ref, normw_ref,
                    o_ref, state_ref):
    step = pl.program_id(0)

    @pl.when(step == 0)
    def _():
        state_ref[...] = jnp.zeros_like(state_ref)

    batch = u_ref.shape[0]
    normw = normw_ref[...]
    chains = [(b, h, slice(h * DN_HEAD_DIM, (h + 1) * DN_HEAD_DIM))
              for b in range(batch) for h in range(DN_HEADS)]
    states = [state_ref[b * DN_HEADS + h] for b, h, _ in chains]
    for c in range(DN_SCAN // CHUNK):
        rs = slice(c * CHUNK, (c + 1) * CHUNK)
        ws_qs = [jnp.dot(jnp.concatenate([w_ref[b, rs, hs], qd_ref[b, rs, hs]], axis=0),
                         state.astype(BF16), preferred_element_type=F32)
                 for (b, h, hs), state in zip(chains, states)]
        v_new = [(u_ref[b, rs, hs].astype(F32) - r[:CHUNK]).astype(BF16)
                 for (b, h, hs), r in zip(chains, ws_qs)]
        av_kv = [jnp.dot(jnp.concatenate([attn_ref[b, rs, h * CHUNK:(h + 1) * CHUNK],
                                          ket_ref[b, hs, rs]], axis=0),
                         vn, preferred_element_type=F32)
                 for (b, h, hs), vn in zip(chains, v_new)]
        states = [state * gl_ref[b, 0, c:c + 1, hs] + r[CHUNK:]
                  for (b, h, hs), state, r in zip(chains, states, av_kv)]
        for (b, h, hs), r1, r2 in zip(chains, ws_qs, av_kv):
            o = r1[CHUNK:] + r2[:CHUNK]
            o = o * lax.rsqrt(jnp.mean(o * o, axis=-1, keepdims=True) + RMS_EPS)
            o_ref[b, rs, hs] = (o * normw * _silu(z_ref[b, rs, hs].astype(F32))).astype(o_ref.dtype)
    for (b, h, _), state in zip(chains, states):
        state_ref[b * DN_HEADS + h] = state


def _deltanet(qkv, z, ab, abt, conv_w, a_log, dt_bias, dn_norm_w, batch, seq):
    t = batch * seq
    tb = min(DN_BLOCK, seq)
    blocks_per_seq = seq // tb
    prow = jnp.zeros((SUBLANES, LANES), F32).at[0, :DN_HEADS].set(a_log).at[1, :DN_HEADS].set(dt_bias)
    pcol = (jnp.zeros((SUBLANES, LANES), F32)
            .at[:DN_HEADS, 0].set(a_log).at[DN_HEADS:2 * DN_HEADS, 0].set(a_log)
            .at[:DN_HEADS, 1].set(dt_bias).at[DN_HEADS:2 * DN_HEADS, 1].set(dt_bias))

    def rows(width):
        return pl.BlockSpec((tb, width), lambda i: (i, 0))

    def const(shape):
        return pl.BlockSpec(shape, lambda i: (0,) * len(shape))

    u, w, qd, attn, ket, gl = pl.pallas_call(
        functools.partial(_dn_prep_kernel, blocks_per_seq=blocks_per_seq),
        grid=(t // tb,),
        in_specs=[rows(3 * DN_WIDTH),
                  pl.BlockSpec((DN_HALO, 3 * DN_WIDTH),
                               lambda i: (jnp.maximum(i * (tb // DN_HALO) - 1, 0), 0)),
                  rows(LANES), pl.BlockSpec((SUBLANES, tb), lambda i: (0, i)),
                  const((CONV_WIDTH, 3 * DN_WIDTH)), const((SUBLANES, LANES)), const((SUBLANES, LANES))],
        out_specs=[rows(DN_WIDTH), rows(DN_WIDTH), rows(DN_WIDTH), rows(DN_HEADS * CHUNK),
                   pl.BlockSpec((1, DN_WIDTH, tb), lambda i: (i // blocks_per_seq, 0, i % blocks_per_seq)),
                   pl.BlockSpec((tb // CHUNK, DN_WIDTH), lambda i: (i, 0))],
        out_shape=[jax.ShapeDtypeStruct((t, DN_WIDTH), BF16)] * 3
        + [jax.ShapeDtypeStruct((t, DN_HEADS * CHUNK), BF16),
           jax.ShapeDtypeStruct((batch, DN_WIDTH, seq), BF16),
           jax.ShapeDtypeStruct((t // CHUNK, DN_WIDTH), F32)],
        scratch_shapes=[pltpu.VMEM((DN_HALO + tb, 3 * DN_WIDTH), BF16)],
        compiler_params=_cparams(("parallel",)),
        name="deltanet_prep",
    )(qkv, qkv, ab, abt, conv_w, prow, pcol)

    def seq_rows(width):
        return pl.BlockSpec((batch, DN_SCAN, width), lambda c: (0, c, 0))

    def view(a):
        return a.reshape(batch, seq, a.shape[-1])

    o = pl.pallas_call(
        _dn_scan_kernel,
        grid=(seq // DN_SCAN,),
        in_specs=[seq_rows(DN_WIDTH), seq_rows(DN_WIDTH), seq_rows(DN_WIDTH), seq_rows(DN_HEADS * CHUNK),
                  pl.BlockSpec((batch, DN_WIDTH, DN_SCAN), lambda c: (0, 0, c)),
                  seq_rows(DN_WIDTH),
                  pl.BlockSpec((batch, 1, DN_SCAN // CHUNK, DN_WIDTH), lambda c: (0, c, 0, 0)),
                  pl.BlockSpec((1, DN_HEAD_DIM), lambda c: (0, 0))],
        out_specs=seq_rows(DN_WIDTH),
        out_shape=jax.ShapeDtypeStruct((batch, seq, DN_WIDTH), BF16),
        scratch_shapes=[pltpu.VMEM((batch * DN_HEADS, DN_HEAD_DIM, DN_HEAD_DIM), F32)],
        compiler_params=_cparams(("arbitrary",)),
        name="deltanet_scan",
    )(view(u), view(w), view(qd), view(attn), ket, view(z),
      gl.reshape(batch, seq // DN_SCAN, DN_SCAN // CHUNK, DN_WIDTH), dn_norm_w.reshape(1, DN_HEAD_DIM))
    return o.reshape(t, DN_WIDTH)


def _swa_kernel(q_ref, ktc_ref, ktp_ref, vc_ref, vp_ref, sink_ref, o_ref):
    qi = lax.broadcasted_iota(jnp.int32, (WINDOW, WINDOW), 0)
    kj = lax.broadcasted_iota(jnp.int32, (WINDOW, WINDOW), 1)
    from_prev = kj > qi
    no_prev = from_prev & (pl.program_id(1) == 0)
    kt = jnp.concatenate([ktp_ref[...], ktc_ref[...]], axis=1)
    v = jnp.concatenate([vp_ref[...], vc_ref[...]], axis=0)
    scores = []
    for j in range(SWA_BLOCKS):
        for hq in range(SW_Q_HEADS):
            hk = hq // SW_GROUP
            q = q_ref[j * WINDOW:(j + 1) * WINDOW, hq * SW_HEAD_DIM:(hq + 1) * SW_HEAD_DIM]
            s = jnp.dot(q, kt[hk * SW_HEAD_DIM:(hk + 1) * SW_HEAD_DIM, j * WINDOW:(j + 2) * WINDOW],
                        preferred_element_type=F32)
            s = jnp.where(from_prev, s[:, :WINDOW], s[:, WINDOW:])
            scores.append(jnp.where(no_prev, -jnp.inf, s) if j == 0 else s)
    s = jnp.concatenate(scores, axis=0)
    sink = sink_ref[...]
    m = jnp.maximum(jnp.max(s, axis=-1, keepdims=True), sink)
    p = jnp.exp(s - m).astype(BF16)
    sink_term = jnp.exp(sink - m)
    zero = jnp.zeros((WINDOW, WINDOW), BF16)
    ones = jnp.ones((WINDOW, SW_HEAD_DIM), BF16)
    v_ones = [[jnp.concatenate([v[jb * WINDOW:(jb + 1) * WINDOW, hk * SW_HEAD_DIM:(hk + 1) * SW_HEAD_DIM],
                                ones], axis=1) for hk in range(SW_KV_HEADS)]
              for jb in range(SWA_BLOCKS + 1)]
    pvs = []
    for j in range(SWA_BLOCKS):
        for hq in range(SW_Q_HEADS):
            hk = hq // SW_GROUP
            rs = slice((j * SW_Q_HEADS + hq) * WINDOW, (j * SW_Q_HEADS + hq + 1) * WINDOW)
            p_prev = jnp.where(from_prev, p[rs], zero)
            p_cur = jnp.where(from_prev, zero, p[rs])
            pvs.append(jnp.dot(p_prev, v_ones[j][hk], preferred_element_type=F32)
                       + jnp.dot(p_cur, v_ones[j + 1][hk], preferred_element_type=F32))
    pv = jnp.concatenate(pvs, axis=0)
    out = pv[:, :SW_HEAD_DIM] / (pv[:, SW_HEAD_DIM:SW_HEAD_DIM + 1] + sink_term)
    for j in range(SWA_BLOCKS):
        heads = [out[(j * SW_Q_HEADS + hq) * WINDOW:(j * SW_Q_HEADS + hq + 1) * WINDOW]
                 for hq in range(SW_Q_HEADS)]
        o_ref[j * WINDOW:(j + 1) * WINDOW, :] = jnp.concatenate(heads, axis=1).astype(o_ref.dtype)


def _swa(qsw, ktsw, vsw, sinks, batch, seq):
    t = batch * seq
    tq = SWA_BLOCKS * WINDOW
    nstep = seq // tq
    sink_rows = jnp.tile(jnp.repeat(sinks.astype(F32), WINDOW), SWA_BLOCKS).reshape(-1, 1)

    def cur(b, i):
        return b * nstep + i

    def prev(b, i):
        return (b * nstep + i) * SWA_BLOCKS - jnp.minimum(i, 1)

    return pl.pallas_call(
        _swa_kernel,
        grid=(batch, nstep),
        in_specs=[pl.BlockSpec((tq, SW_WIDTH), lambda b, i: (cur(b, i), 0)),
                  pl.BlockSpec((SW_KV_WIDTH, tq), lambda b, i: (0, cur(b, i))),
                  pl.BlockSpec((SW_KV_WIDTH, WINDOW), lambda b, i: (0, prev(b, i))),
                  pl.BlockSpec((tq, SW_KV_WIDTH), lambda b, i: (cur(b, i), 0)),
                  pl.BlockSpec((WINDOW, SW_KV_WIDTH), lambda b, i: (prev(b, i), 0)),
                  pl.BlockSpec((SWA_BLOCKS * SW_Q_HEADS * WINDOW, 1), lambda b, i: (0, 0))],
        out_specs=pl.BlockSpec((tq, SW_WIDTH), lambda b, i: (cur(b, i), 0)),
        out_shape=jax.ShapeDtypeStruct((t, SW_WIDTH), BF16),
        compiler_params=_cparams(("parallel", "parallel")),
        name="sliding_window_attn",
    )(qsw, ktsw, ktsw, vsw, vsw, sink_rows)


def _outproj_kernel(odn_ref, osw_ref, w_ref, x_ref, g_ref, b_ref, o_ref):
    mix = (jnp.dot(odn_ref[...], w_ref[0, 0:DN_WIDTH, :], preferred_element_type=F32)
           + jnp.dot(osw_ref[...], w_ref[0, DN_WIDTH:, :], preferred_element_type=F32))
    o_ref[...] = _layer_norm(ALPHA * x_ref[...] + mix, g_ref[...], b_ref[...])


def _outproj_ln(o_dn, o_sw, w_out_bf16, layer, x2d, g, b):
    t = x2d.shape[0]
    tm = min(TM_OUT, t)

    def rows(width):
        return pl.BlockSpec((tm, width), lambda i: (i, 0))

    vec = pl.BlockSpec((1, D_MODEL), lambda i: (0, 0))
    return pl.pallas_call(
        _outproj_kernel,
        grid=(t // tm,),
        in_specs=[rows(DN_WIDTH), rows(SW_WIDTH),
                  pl.BlockSpec((1, DN_WIDTH + SW_WIDTH, D_MODEL), lambda i: (layer, 0, 0)),
                  rows(D_MODEL), vec, vec],
        out_specs=rows(D_MODEL),
        out_shape=jax.ShapeDtypeStruct((t, D_MODEL), F32),
        compiler_params=_cparams(("parallel",)),
        name="out_proj_ln",
    )(o_dn, o_sw, w_out_bf16, x2d, g.reshape(1, D_MODEL), b.reshape(1, D_MODEL))


def _ffn_kernel(x_ref, wg_ref, wu_ref, wd_ref, g_ref, b_ref, o_ref, xb_ref, acc_ref):
    f = pl.program_id(1)

    @pl.when(f == 0)
    def _():
        xb_ref[...] = x_ref[...].astype(BF16)
        acc_ref[...] = jnp.zeros_like(acc_ref)

    xb = xb_ref[...]
    gate = jnp.dot(xb, wg_ref[0].astype(BF16), preferred_element_type=F32)
    up = jnp.dot(xb, wu_ref[0].astype(BF16), preferred_element_type=F32)
    hid = (_silu(gate) * up).astype(BF16)
    acc_ref[...] += jnp.dot(hid, wd_ref[0].astype(BF16), preferred_element_type=F32)

    @pl.when(f == pl.num_programs(1) - 1)
    def _():
        o_ref[...] = _layer_norm(ALPHA * x_ref[...] + acc_ref[...], g_ref[...], b_ref[...])


def _ffn_ln(x2d, w_gate, w_up, w_down, idx, g, b):
    t = x2d.shape[0]
    tm = min(TM_FFN, t)
    vec = pl.BlockSpec((1, D_MODEL), lambda i, f: (0, 0))
    return pl.pallas_call(
        _ffn_kernel,
        grid=(t // tm, D_FF // TF_FFN),
        in_specs=[pl.BlockSpec((tm, D_MODEL), lambda i, f: (i, 0)),
                  pl.BlockSpec((1, D_MODEL, TF_FFN), lambda i, f: (idx, 0, f)),
                  pl.BlockSpec((1, D_MODEL, TF_FFN), lambda i, f: (idx, 0, f)),
                  pl.BlockSpec((1, TF_FFN, D_MODEL), lambda i, f: (idx, f, 0)),
                  vec, vec],
        out_specs=pl.BlockSpec((tm, D_MODEL), lambda i, f: (i, 0)),
        out_shape=jax.ShapeDtypeStruct((t, D_MODEL), F32),
        scratch_shapes=[pltpu.VMEM((tm, D_MODEL), BF16), pltpu.VMEM((tm, D_MODEL), F32)],
        compiler_params=_cparams(("parallel", "arbitrary")),
        name="dense_swiglu_ln",
    )(x2d, w_gate, w_up, w_down, g.reshape(1, D_MODEL), b.reshape(1, D_MODEL))


R_E1, R_E2, R_RANK1, R_RANK2, R_G1, R_G2 = range(6)


def _router_kernel(x_ref, wr_ref, meta_ref, metat_ref, count_ref, carry_ref):
    i = pl.program_id(0)

    @pl.when(i == 0)
    def _():
        carry_ref[...] = jnp.zeros_like(carry_ref)

    tm = x_ref.shape[0]
    logits = jnp.dot(x_ref[...], wr_ref[...], preferred_element_type=F32,
                     precision=lax.Precision.HIGHEST)
    lane = lax.broadcasted_iota(jnp.int32, logits.shape, 1)
    logits = jnp.where(lane < N_EXPERTS, logits, -jnp.inf)
    m1 = jnp.max(logits, axis=-1, keepdims=True)
    e1 = jnp.min(jnp.where(logits == m1, lane, LANES), axis=-1, keepdims=True)
    rest = jnp.where(lane == e1, -jnp.inf, logits)
    m2 = jnp.max(rest, axis=-1, keepdims=True)
    e2 = jnp.min(jnp.where(rest == m2, lane, LANES), axis=-1, keepdims=True)
    ex = jnp.exp(m2 - m1)
    g1 = 1.0 / (1.0 + ex)
    g2 = ex / (1.0 + ex)

    hit = ((lane == e1) | (lane == e2)).astype(BF16)
    r = lax.broadcasted_iota(jnp.int32, (tm, tm), 0)
    c = lax.broadcasted_iota(jnp.int32, (tm, tm), 1)
    before = jnp.dot((c < r).astype(BF16), hit, preferred_element_type=F32) + carry_ref[0:1, :]
    rank1 = jnp.sum(jnp.where(lane == e1, before, 0.0), axis=-1, keepdims=True)
    rank2 = jnp.sum(jnp.where(lane == e2, before, 0.0), axis=-1, keepdims=True)
    carry_ref[0:1, :] = carry_ref[0:1, :] + jnp.sum(hit.astype(F32), axis=0, keepdims=True)
    count_ref[...] = jnp.broadcast_to(carry_ref[0:1, :], count_ref.shape)

    meta = jnp.zeros((tm, LANES), F32)
    for idx, val in ((R_E1, e1.astype(F32)), (R_E2, e2.astype(F32)), (R_RANK1, rank1),
                     (R_RANK2, rank2), (R_G1, g1), (R_G2, g2)):
        meta = jnp.where(lane == idx, val, meta)
    meta_ref[...] = meta[:, :SUBLANES]
    metat_ref[...] = meta.T[:SUBLANES, :]


def _router(x2d, router_w):
    t = x2d.shape[0]
    tm = min(TM_ROUTE, t)
    wr = jnp.zeros((D_MODEL, LANES), F32).at[:, :N_EXPERTS].set(router_w)
    return pl.pallas_call(
        _router_kernel,
        grid=(t // tm,),
        in_specs=[pl.BlockSpec((tm, D_MODEL), lambda i: (i, 0)),
                  pl.BlockSpec((D_MODEL, LANES), lambda i: (0, 0))],
        out_specs=[pl.BlockSpec((tm, SUBLANES), lambda i: (i, 0)),
                   pl.BlockSpec((SUBLANES, tm), lambda i: (0, i)),
                   pl.BlockSpec((SUBLANES, LANES), lambda i: (0, 0))],
        out_shape=[jax.ShapeDtypeStruct((t, SUBLANES), F32),
                   jax.ShapeDtypeStruct((SUBLANES, t), F32),
                   jax.ShapeDtypeStruct((SUBLANES, LANES), F32)],
        scratch_shapes=[pltpu.VMEM((SUBLANES, LANES), F32)],
        compiler_params=_cparams(("arbitrary",)),
        name="moe_router",
    )(x2d, wr)


def _row_copy(src_ref, src_row, dst_ref, dst_row, sem):
    return pltpu.make_async_copy(src_ref.at[pl.ds(src_row, 1), :], dst_ref.at[pl.ds(dst_row, 1), :], sem)


def _dispatch_kernel(dest1_ref, dest2_ref, x_ref, xs_ref, sem):
    base = pl.program_id(0) * TM_MOVE

    for j in range(TM_MOVE):
        _row_copy(x_ref, j, xs_ref, dest1_ref[base + j], sem).start(priority=0)
        _row_copy(x_ref, j, xs_ref, dest2_ref[base + j], sem).start(priority=1)
    for _ in range(2):
        pltpu.make_async_copy(x_ref, xs_ref.at[pl.ds(0, TM_MOVE), :], sem).wait()


def _dispatch(x2d, dest1, dest2, rows_sorted):
    t = x2d.shape[0]
    return pl.pallas_call(
        _dispatch_kernel,
        grid_spec=pltpu.PrefetchScalarGridSpec(
            num_scalar_prefetch=2,
            grid=(t // TM_MOVE,),
            in_specs=[pl.BlockSpec((TM_MOVE, D_MODEL), lambda i, d1, d2: (i, 0))],
            out_specs=pl.BlockSpec(memory_space=pl.ANY),
            scratch_shapes=[pltpu.SemaphoreType.DMA(())]),
        out_shape=jax.ShapeDtypeStruct((rows_sorted, D_MODEL), F32),
        compiler_params=_cparams(("arbitrary",)),
        name="moe_dispatch",
    )(dest1, dest2, x2d)


def _expert_kernel(vtile_ref, vexpert_ref, vfidx_ref, vlo_ref, vhi_ref, vfirst_ref, vlast_ref,
                   xs_ref, wg_ref, wu_ref, wd_ref, ys_ref, xb_ref, acc_ref, wgb_ref, wub_ref, wdb_ref):
    v = pl.program_id(0)
    f = pl.program_id(1)
    lo = vlo_ref[v]
    hi = vhi_ref[v]
    whole = (lo == 0) & (hi == TM_EXPERT)

    def swiglu_rows(rows, wg, wu, wd):
        xb = xb_ref[rows, :]
        gate = jnp.dot(xb, wg, preferred_element_type=F32)
        up = jnp.dot(xb, wu, preferred_element_type=F32)
        hid = (_silu(gate) * up).astype(BF16)
        acc_ref[rows, :] += jnp.dot(hid, wd, preferred_element_type=F32)

    @pl.when(hi > lo)
    def _():
        @pl.when(f == 0)
        def _():
            rowid = lax.broadcasted_iota(jnp.int32, xs_ref.shape, 0)
            xb_ref[...] = jnp.where((rowid >= lo) & (rowid < hi), xs_ref[...], 0.0).astype(BF16)

        @pl.when((f == 0) & (vfirst_ref[v] == 1))
        def _():
            acc_ref[...] = jnp.zeros_like(acc_ref)

        @pl.when(whole)
        def _():
            swiglu_rows(slice(None), wg_ref[0, 0].astype(BF16), wu_ref[0, 0].astype(BF16),
                        wd_ref[0, 0].astype(BF16))

        @pl.when(jnp.logical_not(whole))
        def _():
            wgb_ref[...] = wg_ref[0, 0].astype(BF16)
            wub_ref[...] = wu_ref[0, 0].astype(BF16)
            wdb_ref[...] = wd_ref[0, 0].astype(BF16)
            for sb in range(TM_EXPERT // TM_EXPERT_SUB):
                @pl.when((sb * TM_EXPERT_SUB < hi) & ((sb + 1) * TM_EXPERT_SUB > lo))
                def _():
                    swiglu_rows(slice(sb * TM_EXPERT_SUB, (sb + 1) * TM_EXPERT_SUB),
                                wgb_ref[...], wub_ref[...], wdb_ref[...])

        @pl.when((f == pl.num_programs(1) - 1) & (vlast_ref[v] == 1))
        def _():
            ys_ref[...] = acc_ref[...]


def _expert_visits(counts, n_tiles):
    n_visits = n_tiles + N_EXPERTS - 1
    ends = jnp.cumsum(counts)
    offs = ends - counts
    first_tile = offs // TM_EXPERT
    per_expert = jnp.where(counts > 0, (ends - 1) // TM_EXPERT - first_tile + 1, 0)
    vend = jnp.cumsum(per_expert)
    vstart = vend - per_expert
    total = vend[-1]
    vid = jnp.arange(n_visits, dtype=jnp.int32)
    real = vid < total
    vid_c = jnp.minimum(vid, jnp.maximum(total - 1, 0))
    vexpert = jnp.minimum(jnp.sum(vid_c[:, None] >= vend[None, :], axis=1), N_EXPERTS - 1)
    vtile = first_tile[vexpert] + vid_c - vstart[vexpert]
    lo = jnp.maximum(offs[vexpert], vtile * TM_EXPERT) - vtile * TM_EXPERT
    hi = jnp.minimum(ends[vexpert], (vtile + 1) * TM_EXPERT) - vtile * TM_EXPERT
    prev_tile = jnp.concatenate([jnp.full((1,), -1, vtile.dtype), vtile[:-1]])
    next_tile = jnp.concatenate([vtile[1:], jnp.full((1,), -1, vtile.dtype)])
    first = real & (vtile != prev_tile)
    last = real & ((vtile != next_tile) | (vid == total - 1))
    i32 = lambda a: a.astype(jnp.int32)
    fpin = jnp.where(real, -1, D_FF // TF_FFN - 1)
    return (i32(vtile), i32(vexpert), i32(fpin), i32(jnp.where(real, lo, 0)),
            i32(jnp.where(real, hi, 0)), i32(first), i32(last), i32(offs))


def _expert_ffn(xs, visits, w_gate, w_up, w_down, idx):
    rows_sorted = xs.shape[0]
    n_visits = visits[0].shape[0]
    nf = D_FF // TF_FFN

    def f_idx(v, f, vf):
        return jnp.where(vf[v] >= 0, vf[v], f)

    def rows_map(v, f, vt, ve, vf, *_):
        return (vt[v], 0)

    def wcol_map(v, f, vt, ve, vf, *_):
        return (idx, ve[v], 0, f_idx(v, f, vf))

    def wrow_map(v, f, vt, ve, vf, *_):
        return (idx, ve[v], f_idx(v, f, vf), 0)

    return pl.pallas_call(
        _expert_kernel,
        grid_spec=pltpu.PrefetchScalarGridSpec(
            num_scalar_prefetch=len(visits),
            grid=(n_visits, nf),
            in_specs=[pl.BlockSpec((TM_EXPERT, D_MODEL), rows_map),
                      pl.BlockSpec((1, 1, D_MODEL, TF_FFN), wcol_map),
                      pl.BlockSpec((1, 1, D_MODEL, TF_FFN), wcol_map),
                      pl.BlockSpec((1, 1, TF_FFN, D_MODEL), wrow_map)],
            out_specs=pl.BlockSpec((TM_EXPERT, D_MODEL), rows_map),
            scratch_shapes=[pltpu.VMEM((TM_EXPERT, D_MODEL), BF16),
                            pltpu.VMEM((TM_EXPERT, D_MODEL), F32),
                            pltpu.VMEM((D_MODEL, TF_FFN), BF16),
                            pltpu.VMEM((D_MODEL, TF_FFN), BF16),
                            pltpu.VMEM((TF_FFN, D_MODEL), BF16)]),
        out_shape=jax.ShapeDtypeStruct((rows_sorted, D_MODEL), F32),
        compiler_params=_cparams(("arbitrary", "arbitrary")),
        name="moe_expert_swiglu",
    )(*visits, xs, w_gate, w_up, w_down)


def _combine_kernel(dest1_ref, dest2_ref, ys_ref, x_ref, meta_ref, g_ref, b_ref, o_ref,
                    buf_ref, sems):
    base = pl.program_id(0) * (COMBINE_SUBS * TM_GATHER)

    def gather(k, wait):
        slot = k % COMBINE_SLOTS
        if wait:
            for c in range(2):
                pltpu.make_async_copy(ys_ref.at[pl.ds(0, TM_GATHER), :], buf_ref.at[slot, c],
                                      sems.at[slot]).wait()
            return
        for j in range(TM_GATHER):
            tok = base + k * TM_GATHER + j
            _row_copy(ys_ref, dest1_ref[tok], buf_ref.at[slot, 0], j, sems.at[slot]).start(priority=0)
            _row_copy(ys_ref, dest2_ref[tok], buf_ref.at[slot, 1], j, sems.at[slot]).start(priority=1)

    for k in range(min(COMBINE_SLOTS - 1, COMBINE_SUBS)):
        gather(k, wait=False)
    for k in range(COMBINE_SUBS):
        gather(k, wait=True)
        if k + COMBINE_SLOTS - 1 < COMBINE_SUBS:
            gather(k + COMBINE_SLOTS - 1, wait=False)
        rs = slice(k * TM_GATHER, (k + 1) * TM_GATHER)
        slot = k % COMBINE_SLOTS
        meta = meta_ref[rs, :]
        moe = meta[:, R_G1:R_G1 + 1] * buf_ref[slot, 0] + meta[:, R_G2:R_G2 + 1] * buf_ref[slot, 1]
        o_ref[rs, :] = _layer_norm(ALPHA * x_ref[rs, :] + moe, g_ref[...], b_ref[...])


def _combine_ln(ys, dest1, dest2, x2d, meta, g, b):
    t = x2d.shape[0]
    tm = COMBINE_SUBS * TM_GATHER
    vec = pl.BlockSpec((1, D_MODEL), lambda i, d1, d2: (0, 0))
    return pl.pallas_call(
        _combine_kernel,
        grid_spec=pltpu.PrefetchScalarGridSpec(
            num_scalar_prefetch=2,
            grid=(t // tm,),
            in_specs=[pl.BlockSpec(memory_space=pl.ANY),
                      pl.BlockSpec((tm, D_MODEL), lambda i, d1, d2: (i, 0)),
                      pl.BlockSpec((tm, SUBLANES), lambda i, d1, d2: (i, 0)),
                      vec, vec],
            out_specs=pl.BlockSpec((tm, D_MODEL), lambda i, d1, d2: (i, 0)),
            scratch_shapes=[pltpu.VMEM((COMBINE_SLOTS, 2, TM_GATHER, D_MODEL), F32),
                            pltpu.SemaphoreType.DMA((COMBINE_SLOTS,))]),
        out_shape=jax.ShapeDtypeStruct((t, D_MODEL), F32),
        compiler_params=_cparams(("arbitrary",)),
        name="moe_combine_ln",
    )(dest1, dest2, ys, x2d, meta, g.reshape(1, D_MODEL), b.reshape(1, D_MODEL))


def _moe_ln(x2d, router_w, w_gate, w_up, w_down, idx, g, b):
    t = x2d.shape[0]
    meta, meta_t, counts = _router(x2d, router_w)
    counts = counts[0, :N_EXPERTS].astype(jnp.int32)
    *visits, group_start = _expert_visits(counts, (2 * t) // TM_EXPERT)

    def slot(expert_row, rank_row):
        expert = meta_t[expert_row].astype(jnp.int32)
        start = sum(jnp.where(expert == e, group_start[e], 0) for e in range(N_EXPERTS))
        return start + meta_t[rank_row].astype(jnp.int32)

    dest1 = slot(R_E1, R_RANK1)
    dest2 = slot(R_E2, R_RANK2)
    xs = _dispatch(x2d, dest1, dest2, 2 * t)
    ys = _expert_ffn(xs, tuple(visits), w_gate, w_up, w_down, idx)
    return _combine_ln(ys, dest1, dest2, x2d, meta, g, b)


def _pack_w_in(w):
    dn = 4 * DN_WIDTH
    ab = w[..., dn:dn + 2 * DN_HEADS]
    pad = jnp.zeros(w.shape[:-1] + (LANES - 2 * DN_HEADS,), w.dtype)
    packed = jnp.concatenate([w[..., :dn], w[..., dn + 2 * DN_HEADS:], ab, pad], axis=-1).astype(BF16)
    return packed, jnp.swapaxes(ab, -1, -2).astype(BF16)


@jax.jit
def _trunk(x, positions, w_in, conv_w, a_log, dt_bias, dn_norm_w, sinks, w_out, ln_g, ln_b,
           ffn_w_gate, ffn_w_up, ffn_w_down, router_w, moe_w_gate, moe_w_up, moe_w_down):
    batch, seq, _ = x.shape
    x2d = x.reshape(batch * seq, D_MODEL)
    cos, sin = _rope_tables(positions)
    w_packed, wabt = _pack_w_in(w_in)
    w_out_bf16 = w_out.astype(BF16)
    for layer in range(DEPTH):
        qkv, z, qsw, ktsw, vsw, ab, abt = _inproj(x2d, w_packed, wabt, layer, cos, sin)
        o_dn = _deltanet(qkv, z, ab, abt, conv_w[layer], a_log[layer], dt_bias[layer],
                         dn_norm_w[layer], batch, seq)
        o_sw = _swa(qsw, ktsw, vsw, sinks[layer], batch, seq)
        x2d = _outproj_ln(o_dn, o_sw, w_out_bf16, layer, x2d, ln_g[layer, 0], ln_b[layer, 0])
        i = layer // 2
        if layer % 2 == 0:
            x2d = _ffn_ln(x2d, ffn_w_gate, ffn_w_up, ffn_w_down, i, ln_g[layer, 1], ln_b[layer, 1])
        else:
            x2d = _moe_ln(x2d, router_w[i], moe_w_gate, moe_w_up, moe_w_down, i,
                          ln_g[layer, 1], ln_b[layer, 1])
    return x2d.reshape(batch, seq, D_MODEL)


def kernel(x, positions, w_in, conv_w, a_log, dt_bias, dn_norm_w, sinks, w_out, ln_g, ln_b,
           ffn_w_gate, ffn_w_up, ffn_w_down, router_w, moe_w_gate, moe_w_up, moe_w_down):
    return _trunk(x, positions, w_in, conv_w, a_log, dt_bias, dn_norm_w, sinks, w_out, ln_g, ln_b,
                  ffn_w_gate, ffn_w_up, ffn_w_down, router_w, moe_w_gate, moe_w_up, moe_w_down)
```

```python
import functools

import jax
import jax.numpy as jnp
import numpy as np
from jax import lax
from jax.experimental import pallas as pl
from jax.experimental.pallas import tpu as pltpu

F32 = jnp.float32
BF16 = jnp.bfloat16

D_MODEL = 1024
DEPTH = 4
DN_HEADS = 4
DN_HEAD_DIM = 128
DN_WIDTH = DN_HEADS * DN_HEAD_DIM
CONV_WIDTH = 4
CHUNK = 64
SW_Q_HEADS = 8
SW_KV_HEADS = 2
SW_GROUP = SW_Q_HEADS // SW_KV_HEADS
SW_HEAD_DIM = 64
SW_WIDTH = SW_Q_HEADS * SW_HEAD_DIM
SW_KV_WIDTH = SW_KV_HEADS * SW_HEAD_DIM
WINDOW = 128
ROPE_THETA = 10000.0
D_FF = 3584
N_EXPERTS = 8
ALPHA = (2.0 * DEPTH) ** 0.25
LN_EPS = 1e-5
RMS_EPS = 1e-6

LANES = 128
SUBLANES = 8
DN_HALO = 16
DN_BLOCK = 512
DN_SUB = 128
DN_SCAN = 128
SWA_BLOCKS = 4
VMEM_LIMIT = 56 * 1024 * 1024

C_QKV = (0, 3 * DN_WIDTH)
C_Z = (C_QKV[1], C_QKV[1] + DN_WIDTH)
C_QSW = (C_Z[1], C_Z[1] + SW_WIDTH)
C_KVSW = (C_QSW[1], C_QSW[1] + 2 * SW_KV_WIDTH)
C_AB = (C_KVSW[1], C_KVSW[1] + LANES)
IN_COLS_PACKED = C_AB[1]

TM_PROJ = 512
TM_ROPE = 2048
TM_FFN = 1024
TF_FFN = 512
TM_ROUTE = 1024
TM_EXPERT = 2048
TF_EXPERT = 256
VMEM_LIMIT_EXPERT = 62 * 1024 * 1024
TM_EXPERT_SUB = 256
TM_OUT = 1024
TM_MOVE = 512
TM_GATHER = 256
COMBINE_SUBS = 8
COMBINE_SLOTS = 3


def _cparams(sem):
    return pltpu.CompilerParams(dimension_semantics=sem, vmem_limit_bytes=VMEM_LIMIT)


def _bdot(a, b):
    return jnp.dot(a.astype(BF16), b.astype(BF16), preferred_element_type=F32)


def _bdot_nt(a, b):
    return lax.dot_general(a.astype(BF16), b.astype(BF16), (((1,), (1,)), ((), ())),
                           preferred_element_type=F32)


def _silu(x):
    return x * jax.nn.sigmoid(x)


def _softplus(x):
    return jnp.maximum(x, 0.0) + jnp.log1p(jnp.exp(-jnp.abs(x)))


def _layer_norm(y, g, b):
    mu = jnp.mean(y, axis=-1, keepdims=True)
    d = y - mu
    var = jnp.mean(d * d, axis=-1, keepdims=True)
    return d * lax.rsqrt(var + LN_EPS) * g + b


def _rope_table_kernel(pos_ref, invf_ref, sign_ref, cos_ref, sin_ref):
    ang = pos_ref[...].astype(F32) * invf_ref[...]
    cos_ref[...] = jnp.cos(ang)
    sin_ref[...] = jnp.sin(ang) * sign_ref[...]


def _rope_tables(positions):
    t = positions.size
    half = SW_HEAD_DIM // 2
    inv_freq = ROPE_THETA ** (-jnp.arange(0, SW_HEAD_DIM, 2, dtype=F32) / SW_HEAD_DIM)
    reps = LANES // half
    invf = jnp.tile(inv_freq, reps).reshape(1, LANES)
    sign = jnp.tile(jnp.concatenate([-jnp.ones((half,), F32), jnp.ones((half,), F32)]),
                    LANES // SW_HEAD_DIM).reshape(1, LANES)
    tm = min(TM_ROPE, t)
    row = pl.BlockSpec((1, LANES), lambda i: (0, 0))
    tab = pl.BlockSpec((tm, LANES), lambda i: (i, 0))
    return pl.pallas_call(
        _rope_table_kernel,
        grid=(t // tm,),
        in_specs=[pl.BlockSpec((tm, 1), lambda i: (i, 0)), row, row],
        out_specs=[tab, tab],
        out_shape=[jax.ShapeDtypeStruct((t, LANES), F32)] * 2,
        compiler_params=_cparams(("parallel",)),
        name="rope_tables",
    )(positions.reshape(t, 1), invf, sign)


def _rope(x, cos, sin_signed):
    width = x.shape[-1]
    half = SW_HEAD_DIM // 2
    lane = lax.broadcasted_iota(jnp.int32, x.shape, 1)
    first = (lane % SW_HEAD_DIM) < half
    partner = jnp.where(first, pltpu.roll(x, width - half, 1), pltpu.roll(x, half, 1))
    return x * cos + partner * sin_signed


def _inproj_kernel(x_ref, w_ref, wabt_ref, cos_ref, sin_ref,
                   qkv_ref, z_ref, qsw_ref, ktsw_ref, vsw_ref, ab_ref, abt_ref):
    xb = x_ref[...].astype(BF16)

    def mm(cols):
        return jnp.dot(xb, w_ref[0, :, cols[0]:cols[1]], preferred_element_type=F32)

    qkv_ref[...] = mm(C_QKV).astype(BF16)
    z_ref[...] = mm(C_Z).astype(BF16)
    cos = cos_ref[...]
    sin = sin_ref[...]
    reps = SW_WIDTH // LANES
    q = _rope(mm(C_QSW), jnp.tile(cos, (1, reps)), jnp.tile(sin, (1, reps))) * (SW_HEAD_DIM ** -0.5)
    qsw_ref[...] = q.astype(BF16)
    kv = mm(C_KVSW)
    ktsw_ref[...] = _rope(kv[:, :SW_KV_WIDTH], cos, sin).T.astype(BF16)
    vsw_ref[...] = kv[:, SW_KV_WIDTH:].astype(BF16)
    ab_ref[...] = mm(C_AB)
    abt_ref[...] = lax.dot_general(wabt_ref[0], xb, (((1,), (1,)), ((), ())),
                                   preferred_element_type=F32)


def _inproj(x2d, w_packed, wabt, layer, cos, sin):
    t = x2d.shape[0]
    tm = min(TM_PROJ, t)

    def rows(width):
        return pl.BlockSpec((tm, width), lambda i: (i, 0))

    return pl.pallas_call(
        _inproj_kernel,
        grid=(t // tm,),
        in_specs=[rows(D_MODEL),
                  pl.BlockSpec((1, D_MODEL, IN_COLS_PACKED), lambda i: (layer, 0, 0)),
                  pl.BlockSpec((1, SUBLANES, D_MODEL), lambda i: (layer, 0, 0)),
                  rows(LANES), rows(LANES)],
        out_specs=[rows(3 * DN_WIDTH), rows(DN_WIDTH), rows(SW_WIDTH),
                   pl.BlockSpec((SW_KV_WIDTH, tm), lambda i: (0, i)), rows(SW_KV_WIDTH),
                   rows(LANES), pl.BlockSpec((SUBLANES, tm), lambda i: (0, i))],
        out_shape=[jax.ShapeDtypeStruct((t, 3 * DN_WIDTH), BF16),
                   jax.ShapeDtypeStruct((t, DN_WIDTH), BF16),
                   jax.ShapeDtypeStruct((t, SW_WIDTH), BF16),
                   jax.ShapeDtypeStruct((SW_KV_WIDTH, t), BF16),
                   jax.ShapeDtypeStruct((t, SW_KV_WIDTH), BF16),
                   jax.ShapeDtypeStruct((t, LANES), F32),
                   jax.ShapeDtypeStruct((SUBLANES, t), F32)],
        compiler_params=_cparams(("parallel",)),
        name="in_proj",
    )(x2d, w_packed, wabt, cos, sin)


def _bmm(a, b):
    return lax.dot_general(a, b, (((2,), (1,)), ((0,), (0,))), preferred_element_type=F32)


def _bmm_nt(a, b):
    return lax.dot_general(a, b, (((2,), (2,)), ((0,), (0,))), preferred_element_type=F32)


def _unit_lower_inverse(lmat, row, col):
    same16 = ((row // 16) == (col // 16))[None]
    same32 = ((row // 32) == (col // 32))[None]
    eye = (row == col).astype(F32)[None]
    ld = jnp.where(same16, lmat, 0.0)
    inv = eye - ld
    power = ld.astype(BF16)
    for _ in range(3):
        power = _bmm(power, power).astype(BF16)
        inv = inv + _bmm(inv.astype(BF16), power)
    lmat_bf16 = lmat.astype(BF16)
    for off_diag in (same32 & ~same16, ~same32):
        c = jnp.where(off_diag, lmat_bf16, jnp.zeros_like(lmat_bf16))
        inv_bf16 = inv.astype(BF16)
        inv = inv - _bmm(_bmm(inv_bf16, c).astype(BF16), inv_bf16)
    return inv


def _dn_prep_kernel(qkv_ref, halo_ref, ab_ref, abt_ref, convw_ref, prow_ref, pcol_ref,
                    u_ref, w_ref, qd_ref, attn_ref, ket_ref, gl_ref, xp_ref, *, blocks_per_seq):
    tb = qkv_ref.shape[0]
    first = (pl.program_id(0) % blocks_per_seq) == 0

    halo = halo_ref[...]
    xp_ref[0:DN_HALO, :] = jnp.where(first, jnp.zeros_like(halo), halo)
    xp_ref[DN_HALO:DN_HALO + tb, :] = qkv_ref[...]
    convw = convw_ref[...]
    span = DN_HALO + DN_SUB
    out_row = lax.broadcasted_iota(jnp.int32, ((CONV_WIDTH - 1) * DN_SUB, span), 0)
    src_row = lax.broadcasted_iota(jnp.int32, ((CONV_WIDTH - 1) * DN_SUB, span), 1)
    back = (CONV_WIDTH - 1) - out_row // DN_SUB
    select = (src_row == DN_HALO + out_row % DN_SUB - back).astype(BF16)
    tiles = []
    for r in range(tb // DN_SUB):
        window = xp_ref[r * DN_SUB:r * DN_SUB + span, :]
        shifted = jnp.dot(select, window, preferred_element_type=F32)
        y = window[DN_HALO:, :].astype(F32) * convw[CONV_WIDTH - 1:CONV_WIDTH, :]
        for tap in range(CONV_WIDTH - 1):
            y = y + shifted[tap * DN_SUB:(tap + 1) * DN_SUB, :] * convw[tap:tap + 1, :]
        tiles.append(y)
    qkv = _silu(jnp.concatenate(tiles, axis=0))

    row = lax.broadcasted_iota(jnp.int32, (DN_SUB, DN_SUB), 0)
    col = lax.broadcasted_iota(jnp.int32, (DN_SUB, DN_SUB), 1)
    same_chunk = (row // CHUNK) == (col // CHUNK)
    lower_incl = same_chunk & (row >= col)
    strict_lower = same_chunk & (row > col)
    cum_lhs = jnp.concatenate([lower_incl.astype(F32), same_chunk.astype(F32)], axis=0)
    cum_rhs_t = (same_chunk & (row <= col)).astype(F32)

    ab = ab_ref[...]
    g_cols = -jnp.exp(prow_ref[0:1, :]) * _softplus(ab + prow_ref[1:2, :])
    g_rows = -jnp.exp(pcol_ref[:, 0:1]) * _softplus(abt_ref[...] + pcol_ref[:, 1:2])
    beta_cols = jax.nn.sigmoid(ab)

    nsub = tb // DN_SUB
    subs = [slice(s * DN_SUB, (s + 1) * DN_SUB) for s in range(nsub)]
    gl_rows, gcum_cols, gtot_cols, gcum_rows = [], [], [], []
    for rs in subs:
        cum = jnp.dot(cum_lhs, g_cols[rs, :], preferred_element_type=F32,
                      precision=lax.Precision.HIGHEST)
        gcum_cols.append(cum[:DN_SUB])
        gtot_cols.append(cum[DN_SUB:])
        gcum_rows.append(jnp.dot(g_rows[:, rs], cum_rhs_t, preferred_element_type=F32,
                                 precision=lax.Precision.HIGHEST))
        for c in range(DN_SUB // CHUNK):
            tot = jnp.exp(cum[DN_SUB + c * CHUNK:DN_SUB + c * CHUNK + 1, :])
            gl_rows.append(jnp.concatenate(
                [jnp.broadcast_to(tot[:, h:h + 1], (1, DN_HEAD_DIM)) for h in range(DN_HEADS)], axis=1))
    gl_ref[...] = jnp.concatenate(gl_rows, axis=0)

    chains = [(s, h) for s in range(nsub) for h in range(DN_HEADS)]

    def stack(fn):
        return jnp.stack([fn(s, h) for s, h in chains], axis=0)

    def head_cols(base):
        return stack(lambda s, h: qkv[subs[s], base + h * DN_HEAD_DIM:base + (h + 1) * DN_HEAD_DIM])

    q = head_cols(0)
    k = head_cols(DN_WIDTH)
    v = head_cols(2 * DN_WIDTH)
    gc = stack(lambda s, h: gcum_cols[s][:, h:h + 1])
    gt = stack(lambda s, h: gtot_cols[s][:, h:h + 1])
    gr = stack(lambda s, h: gcum_rows[s][h:h + 1, :])
    beta = stack(lambda s, h: beta_cols[subs[s], DN_HEADS + h:DN_HEADS + h + 1])

    q = q * lax.rsqrt(jnp.sum(q * q, axis=-1, keepdims=True) + 1e-6) * (DN_HEAD_DIM ** -0.5)
    k = k * lax.rsqrt(jnp.sum(k * k, axis=-1, keepdims=True) + 1e-6)
    lower3 = lower_incl[None]
    decay = jnp.where(lower3, jnp.exp(jnp.where(lower3, gc - gr, 0.0)), 0.0)
    k_beta = k * beta
    gram = _bmm_nt(jnp.concatenate([q, k_beta], axis=1).astype(BF16), k.astype(BF16))
    attn = jnp.where(lower3, gram[:, :DN_SUB] * decay, 0.0)
    lmat = jnp.where(strict_lower[None], gram[:, DN_SUB:] * decay, 0.0)
    tinv = _unit_lower_inverse(lmat, row, col)
    exp_gc = jnp.exp(gc)
    uw = _bmm(tinv.astype(BF16), jnp.concatenate([v * beta, k_beta * exp_gc], axis=2).astype(BF16))
    q_dec = (q * exp_gc).astype(BF16)
    k_end = k * jnp.exp(gt - gc)
    compact = attn[:, :, 0:CHUNK]
    for c in range(1, DN_SUB // CHUNK):
        compact = compact + attn[:, :, c * CHUNK:(c + 1) * CHUNK]
    compact = compact.astype(BF16)
    uw = uw.astype(BF16)

    for idx, (s, h) in enumerate(chains):
        rs = subs[s]
        hs = slice(h * DN_HEAD_DIM, (h + 1) * DN_HEAD_DIM)
        u_ref[rs, hs] = uw[idx, :, :DN_HEAD_DIM]
        w_ref[rs, hs] = uw[idx, :, DN_HEAD_DIM:]
        qd_ref[rs, hs] = q_dec[idx]
        ket_ref[0, hs, rs] = k_end[idx].T.astype(BF16)
        attn_ref[rs, h * CHUNK:(h + 1) * CHUNK] = compact[idx]


def _dn_scan_kernel(u_ref, w_ref, qd_ref, attn_ref, ket_ref, z_ref, gl_ref, normw_ref,
                    o_ref, state_ref):
    step = pl.program_id(0)

    @pl.when(step == 0)
    def _():
        state_ref[...] = jnp.zeros_like(state_ref)

    batch = u_ref.shape[0]
    normw = normw_ref[...]
    chains = [(b, h, slice(h * DN_HEAD_DIM, (h + 1) * DN_HEAD_DIM))
              for b in range(batch) for h in range(DN_HEADS)]
    states = [state_ref[b * DN_HEADS + h] for b, h, _ in chains]
    for c in range(DN_SCAN // CHUNK):
        rs = slice(c * CHUNK, (c + 1) * CHUNK)
        ws_qs = [jnp.dot(jnp.concatenate([w_ref[b, rs, hs], qd_ref[b, rs, hs]], axis=0),
                         state.astype(BF16), preferred_element_type=F32)
                 for (b, h, hs), state in zip(chains, states)]
        v_new = [(u_ref[b, rs, hs].astype(F32) - r[:CHUNK]).astype(BF16)
                 for (b, h, hs), r in zip(chains, ws_qs)]
        av_kv = [jnp.dot(jnp.concatenate([attn_ref[b, rs, h * CHUNK:(h + 1) * CHUNK],
                                          ket_ref[b, hs, rs]], axis=0),
                         vn, preferred_element_type=F32)
                 for (b, h, hs), vn in zip(chains, v_new)]
        states = [state * gl_ref[b, 0, c:c + 1, hs] + r[CHUNK:]
                  for (b, h, hs), state, r in zip(chains, states, av_kv)]
        for (b, h, hs), r1, r2 in zip(chains, ws_qs, av_kv):
            o = r1[CHUNK:] + r2[:CHUNK]
            o = o * lax.rsqrt(jnp.mean(o * o, axis=-1, keepdims=True) + RMS_EPS)
            o_ref[b, rs, hs] = (o * normw * _silu(z_ref[b, rs, hs].astype(F32))).astype(o_ref.dtype)
    for (b, h, _), state in zip(chains, states):
        state_ref[b * DN_HEADS + h] = state


def _deltanet(qkv, z, ab, abt, conv_w, a_log, dt_bias, dn_norm_w, batch, seq):
    t = batch * seq
    tb = min(DN_BLOCK, seq)
    blocks_per_seq = seq // tb
    prow = jnp.zeros((SUBLANES, LANES), F32).at[0, :DN_HEADS].set(a_log).at[1, :DN_HEADS].set(dt_bias)
    pcol = (jnp.zeros((SUBLANES, LANES), F32)
            .at[:DN_HEADS, 0].set(a_log).at[DN_HEADS:2 * DN_HEADS, 0].set(a_log)
            .at[:DN_HEADS, 1].set(dt_bias).at[DN_HEADS:2 * DN_HEADS, 1].set(dt_bias))

    def rows(width):
        return pl.BlockSpec((tb, width), lambda i: (i, 0))

    def const(shape):
        return pl.BlockSpec(shape, lambda i: (0,) * len(shape))

    u, w, qd, attn, ket, gl = pl.pallas_call(
        functools.partial(_dn_prep_kernel, blocks_per_seq=blocks_per_seq),
        grid=(t // tb,),
        in_specs=[rows(3 * DN_WIDTH),
                  pl.BlockSpec((DN_HALO, 3 * DN_WIDTH),
                               lambda i: (jnp.maximum(i * (tb // DN_HALO) - 1, 0), 0)),
                  rows(LANES), pl.BlockSpec((SUBLANES, tb), lambda i: (0, i)),
                  const((CONV_WIDTH, 3 * DN_WIDTH)), const((SUBLANES, LANES)), const((SUBLANES, LANES))],
        out_specs=[rows(DN_WIDTH), rows(DN_WIDTH), rows(DN_WIDTH), rows(DN_HEADS * CHUNK),
                   pl.BlockSpec((1, DN_WIDTH, tb), lambda i: (i // blocks_per_seq, 0, i % blocks_per_seq)),
                   pl.BlockSpec((tb // CHUNK, DN_WIDTH), lambda i: (i, 0))],
        out_shape=[jax.ShapeDtypeStruct((t, DN_WIDTH), BF16)] * 3
        + [jax.ShapeDtypeStruct((t, DN_HEADS * CHUNK), BF16),
           jax.ShapeDtypeStruct((batch, DN_WIDTH, seq), BF16),
           jax.ShapeDtypeStruct((t // CHUNK, DN_WIDTH), F32)],
        scratch_shapes=[pltpu.VMEM((DN_HALO + tb, 3 * DN_WIDTH), BF16)],
        compiler_params=_cparams(("parallel",)),
        name="deltanet_prep",
    )(qkv, qkv, ab, abt, conv_w, prow, pcol)

    def seq_rows(width):
        return pl.BlockSpec((batch, DN_SCAN, width), lambda c: (0, c, 0))

    def view(a):
        return a.reshape(batch, seq, a.shape[-1])

    o = pl.pallas_call(
        _dn_scan_kernel,
        grid=(seq // DN_SCAN,),
        in_specs=[seq_rows(DN_WIDTH), seq_rows(DN_WIDTH), seq_rows(DN_WIDTH), seq_rows(DN_HEADS * CHUNK),
                  pl.BlockSpec((batch, DN_WIDTH, DN_SCAN), lambda c: (0, 0, c)),
                  seq_rows(DN_WIDTH),
                  pl.BlockSpec((batch, 1, DN_SCAN // CHUNK, DN_WIDTH), lambda c: (0, c, 0, 0)),
                  pl.BlockSpec((1, DN_HEAD_DIM), lambda c: (0, 0))],
        out_specs=seq_rows(DN_WIDTH),
        out_shape=jax.ShapeDtypeStruct((batch, seq, DN_WIDTH), BF16),
        scratch_shapes=[pltpu.VMEM((batch * DN_HEADS, DN_HEAD_DIM, DN_HEAD_DIM), F32)],
        compiler_params=_cparams(("arbitrary",)),
        name="deltanet_scan",
    )(view(u), view(w), view(qd), view(attn), ket, view(z),
      gl.reshape(batch, seq // DN_SCAN, DN_SCAN // CHUNK, DN_WIDTH), dn_norm_w.reshape(1, DN_HEAD_DIM))
    return o.reshape(t, DN_WIDTH)


def _swa_kernel(q_ref, ktc_ref, ktp_ref, vc_ref, vp_ref, sink_ref, o_ref):
    qi = lax.broadcasted_iota(jnp.int32, (WINDOW, WINDOW), 0)
    kj = lax.broadcasted_iota(jnp.int32, (WINDOW, WINDOW), 1)
    from_prev = kj > qi
    no_prev = from_prev & (pl.program_id(1) == 0)
    kt = jnp.concatenate([ktp_ref[...], ktc_ref[...]], axis=1)
    v = jnp.concatenate([vp_ref[...], vc_ref[...]], axis=0)
    scores = []
    for j in range(SWA_BLOCKS):
        for hq in range(SW_Q_HEADS):
            hk = hq // SW_GROUP
            q = q_ref[j * WINDOW:(j + 1) * WINDOW, hq * SW_HEAD_DIM:(hq + 1) * SW_HEAD_DIM]
            s = jnp.dot(q, kt[hk * SW_HEAD_DIM:(hk + 1) * SW_HEAD_DIM, j * WINDOW:(j + 2) * WINDOW],
                        preferred_element_type=F32)
            s = jnp.where(from_prev, s[:, :WINDOW], s[:, WINDOW:])
            scores.append(jnp.where(no_prev, -jnp.inf, s) if j == 0 else s)
    s = jnp.concatenate(scores, axis=0)
    sink = sink_ref[...]
    m = jnp.maximum(jnp.max(s, axis=-1, keepdims=True), sink)
    p = jnp.exp(s - m).astype(BF16)
    sink_term = jnp.exp(sink - m)
    zero = jnp.zeros((WINDOW, WINDOW), BF16)
    ones = jnp.ones((WINDOW, SW_HEAD_DIM), BF16)
    v_ones = [[jnp.concatenate([v[jb * WINDOW:(jb + 1) * WINDOW, hk * SW_HEAD_DIM:(hk + 1) * SW_HEAD_DIM],
                                ones], axis=1) for hk in range(SW_KV_HEADS)]
              for jb in range(SWA_BLOCKS + 1)]
    pvs = []
    for j in range(SWA_BLOCKS):
        for hq in range(SW_Q_HEADS):
            hk = hq // SW_GROUP
            rs = slice((j * SW_Q_HEADS + hq) * WINDOW, (j * SW_Q_HEADS + hq + 1) * WINDOW)
            p_prev = jnp.where(from_prev, p[rs], zero)
            p_cur = jnp.where(from_prev, zero, p[rs])
            pvs.append(jnp.dot(p_prev, v_ones[j][hk], preferred_element_type=F32)
                       + jnp.dot(p_cur, v_ones[j + 1][hk], preferred_element_type=F32))
    pv = jnp.concatenate(pvs, axis=0)
    out = pv[:, :SW_HEAD_DIM] / (pv[:, SW_HEAD_DIM:SW_HEAD_DIM + 1] + sink_term)
    for j in range(SWA_BLOCKS):
        heads = [out[(j * SW_Q_HEADS + hq) * WINDOW:(j * SW_Q_HEADS + hq + 1) * WINDOW]
                 for hq in range(SW_Q_HEADS)]
        o_ref[j * WINDOW:(j + 1) * WINDOW, :] = jnp.concatenate(heads, axis=1).astype(o_ref.dtype)


def _swa(qsw, ktsw, vsw, sinks, batch, seq):
    t = batch * seq
    tq = SWA_BLOCKS * WINDOW
    nstep = seq // tq
    sink_rows = jnp.tile(jnp.repeat(sinks.astype(F32), WINDOW), SWA_BLOCKS).reshape(-1, 1)

    def cur(b, i):
        return b * nstep + i

    def prev(b, i):
        return (b * nstep + i) * SWA_BLOCKS - jnp.minimum(i, 1)

    return pl.pallas_call(
        _swa_kernel,
        grid=(batch, nstep),
        in_specs=[pl.BlockSpec((tq, SW_WIDTH), lambda b, i: (cur(b, i), 0)),
                  pl.BlockSpec((SW_KV_WIDTH, tq), lambda b, i: (0, cur(b, i))),
                  pl.BlockSpec((SW_KV_WIDTH, WINDOW), lambda b, i: (0, prev(b, i))),
                  pl.BlockSpec((tq, SW_KV_WIDTH), lambda b, i: (cur(b, i), 0)),
                  pl.BlockSpec((WINDOW, SW_KV_WIDTH), lambda b, i: (prev(b, i), 0)),
                  pl.BlockSpec((SWA_BLOCKS * SW_Q_HEADS * WINDOW, 1), lambda b, i: (0, 0))],
        out_specs=pl.BlockSpec((tq, SW_WIDTH), lambda b, i: (cur(b, i), 0)),
        out_shape=jax.ShapeDtypeStruct((t, SW_WIDTH), BF16),
        compiler_params=_cparams(("parallel", "parallel")),
        name="sliding_window_attn",
    )(qsw, ktsw, ktsw, vsw, vsw, sink_rows)


def _outproj_kernel(odn_ref, osw_ref, w_ref, x_ref, g_ref, b_ref, o_ref):
    mix = (jnp.dot(odn_ref[...], w_ref[0, 0:DN_WIDTH, :], preferred_element_type=F32)
           + jnp.dot(osw_ref[...], w_ref[0, DN_WIDTH:, :], preferred_element_type=F32))
    o_ref[...] = _layer_norm(ALPHA * x_ref[...] + mix, g_ref[...], b_ref[...])


def _outproj_ln(o_dn, o_sw, w_out_bf16, layer, x2d, g, b):
    t = x2d.shape[0]
    tm = min(TM_OUT, t)

    def rows(width):
        return pl.BlockSpec((tm, width), lambda i: (i, 0))

    vec = pl.BlockSpec((1, D_MODEL), lambda i: (0, 0))
    return pl.pallas_call(
        _outproj_kernel,
        grid=(t // tm,),
        in_specs=[rows(DN_WIDTH), rows(SW_WIDTH),
                  pl.BlockSpec((1, DN_WIDTH + SW_WIDTH, D_MODEL), lambda i: (layer, 0, 0)),
                  rows(D_MODEL), vec, vec],
        out_specs=rows(D_MODEL),
        out_shape=jax.ShapeDtypeStruct((t, D_MODEL), F32),
        compiler_params=_cparams(("parallel",)),
        name="out_proj_ln",
    )(o_dn, o_sw, w_out_bf16, x2d, g.reshape(1, D_MODEL), b.reshape(1, D_MODEL))


def _ffn_kernel(x_ref, wg_ref, wu_ref, wd_ref, g_ref, b_ref, o_ref, xb_ref, acc_ref):
    f = pl.program_id(1)

    @pl.when(f == 0)
    def _():
        xb_ref[...] = x_ref[...].astype(BF16)
        acc_ref[...] = jnp.zeros_like(acc_ref)

    xb = xb_ref[...]
    gate = jnp.dot(xb, wg_ref[0].astype(BF16), preferred_element_type=F32)
    up = jnp.dot(xb, wu_ref[0].astype(BF16), preferred_element_type=F32)
    hid = (_silu(gate) * up).astype(BF16)
    acc_ref[...] += jnp.dot(hid, wd_ref[0].astype(BF16), preferred_element_type=F32)

    @pl.when(f == pl.num_programs(1) - 1)
    def _():
        o_ref[...] = _layer_norm(ALPHA * x_ref[...] + acc_ref[...], g_ref[...], b_ref[...])


def _ffn_ln(x2d, w_gate, w_up, w_down, idx, g, b):
    t = x2d.shape[0]
    tm = min(TM_FFN, t)
    vec = pl.BlockSpec((1, D_MODEL), lambda i, f: (0, 0))
    return pl.pallas_call(
        _ffn_kernel,
        grid=(t // tm, D_FF // TF_FFN),
        in_specs=[pl.BlockSpec((tm, D_MODEL), lambda i, f: (i, 0)),
                  pl.BlockSpec((1, D_MODEL, TF_FFN), lambda i, f: (idx, 0, f)),
                  pl.BlockSpec((1, D_MODEL, TF_FFN), lambda i, f: (idx, 0, f)),
                  pl.BlockSpec((1, TF_FFN, D_MODEL), lambda i, f: (idx, f, 0)),
                  vec, vec],
        out_specs=pl.BlockSpec((tm, D_MODEL), lambda i, f: (i, 0)),
        out_shape=jax.ShapeDtypeStruct((t, D_MODEL), F32),
        scratch_shapes=[pltpu.VMEM((tm, D_MODEL), BF16), pltpu.VMEM((tm, D_MODEL), F32)],
        compiler_params=_cparams(("parallel", "arbitrary")),
        name="dense_swiglu_ln",
    )(x2d, w_gate, w_up, w_down, g.reshape(1, D_MODEL), b.reshape(1, D_MODEL))


R_E1, R_E2, R_RANK1, R_RANK2, R_G1, R_G2 = range(6)


def _router_kernel(x_ref, wr_ref, meta_ref, metat_ref, count_ref, carry_ref):
    i = pl.program_id(0)

    @pl.when(i == 0)
    def _():
        carry_ref[...] = jnp.zeros_like(carry_ref)

    tm = x_ref.shape[0]
    logits = jnp.dot(x_ref[...], wr_ref[...], preferred_element_type=F32,
                     precision=lax.Precision.HIGHEST)
    lane = lax.broadcasted_iota(jnp.int32, logits.shape, 1)
    logits = jnp.where(lane < N_EXPERTS, logits, -jnp.inf)
    m1 = jnp.max(logits, axis=-1, keepdims=True)
    e1 = jnp.min(jnp.where(logits == m1, lane, LANES), axis=-1, keepdims=True)
    rest = jnp.where(lane == e1, -jnp.inf, logits)
    m2 = jnp.max(rest, axis=-1, keepdims=True)
    e2 = jnp.min(jnp.where(rest == m2, lane, LANES), axis=-1, keepdims=True)
    ex = jnp.exp(m2 - m1)
    g1 = 1.0 / (1.0 + ex)
    g2 = ex / (1.0 + ex)

    hit = ((lane == e1) | (lane == e2)).astype(BF16)
    r = lax.broadcasted_iota(jnp.int32, (tm, tm), 0)
    c = lax.broadcasted_iota(jnp.int32, (tm, tm), 1)
    before = jnp.dot((c < r).astype(BF16), hit, preferred_element_type=F32) + carry_ref[0:1, :]
    rank1 = jnp.sum(jnp.where(lane == e1, before, 0.0), axis=-1, keepdims=True)
    rank2 = jnp.sum(jnp.where(lane == e2, before, 0.0), axis=-1, keepdims=True)
    carry_ref[0:1, :] = carry_ref[0:1, :] + jnp.sum(hit.astype(F32), axis=0, keepdims=True)
    count_ref[...] = jnp.broadcast_to(carry_ref[0:1, :], count_ref.shape)

    meta = jnp.zeros((tm, LANES), F32)
    for idx, val in ((R_E1, e1.astype(F32)), (R_E2, e2.astype(F32)), (R_RANK1, rank1),
                     (R_RANK2, rank2), (R_G1, g1), (R_G2, g2)):
        meta = jnp.where(lane == idx, val, meta)
    meta_ref[...] = meta[:, :SUBLANES]
    metat_ref[...] = meta.T[:SUBLANES, :]


def _router(x2d, router_w):
    t = x2d.shape[0]
    tm = min(TM_ROUTE, t)
    wr = jnp.zeros((D_MODEL, LANES), F32).at[:, :N_EXPERTS].set(router_w)
    return pl.pallas_call(
        _router_kernel,
        grid=(t // tm,),
        in_specs=[pl.BlockSpec((tm, D_MODEL), lambda i: (i, 0)),
                  pl.BlockSpec((D_MODEL, LANES), lambda i: (0, 0))],
        out_specs=[pl.BlockSpec((tm, SUBLANES), lambda i: (i, 0)),
                   pl.BlockSpec((SUBLANES, tm), lambda i: (0, i)),
                   pl.BlockSpec((SUBLANES, LANES), lambda i: (0, 0))],
        out_shape=[jax.ShapeDtypeStruct((t, SUBLANES), F32),
                   jax.ShapeDtypeStruct((SUBLANES, t), F32),
                   jax.ShapeDtypeStruct((SUBLANES, LANES), F32)],
        scratch_shapes=[pltpu.VMEM((SUBLANES, LANES), F32)],
        compiler_params=_cparams(("arbitrary",)),
        name="moe_router",
    )(x2d, wr)


def _row_copy(src_ref, src_row, dst_ref, dst_row, sem):
    return pltpu.make_async_copy(src_ref.at[pl.ds(src_row, 1), :], dst_ref.at[pl.ds(dst_row, 1), :], sem)


def _dispatch_kernel(dest1_ref, dest2_ref, x_ref, xs_ref, sem):
    base = pl.program_id(0) * TM_MOVE

    for j in range(TM_MOVE):
        _row_copy(x_ref, j, xs_ref, dest1_ref[base + j], sem).start(priority=0)
        _row_copy(x_ref, j, xs_ref, dest2_ref[base + j], sem).start(priority=1)
    for _ in range(2):
        pltpu.make_async_copy(x_ref, xs_ref.at[pl.ds(0, TM_MOVE), :], sem).wait()


def _dispatch(x2d, dest1, dest2, rows_sorted):
    t = x2d.shape[0]
    return pl.pallas_call(
        _dispatch_kernel,
        grid_spec=pltpu.PrefetchScalarGridSpec(
            num_scalar_prefetch=2,
            grid=(t // TM_MOVE,),
            in_specs=[pl.BlockSpec((TM_MOVE, D_MODEL), lambda i, d1, d2: (i, 0))],
            out_specs=pl.BlockSpec(memory_space=pl.ANY),
            scratch_shapes=[pltpu.SemaphoreType.DMA(())]),
        out_shape=jax.ShapeDtypeStruct((rows_sorted, D_MODEL), F32),
        compiler_params=_cparams(("arbitrary",)),
        name="moe_dispatch",
    )(dest1, dest2, x2d)


def _expert_kernel(vtile_ref, vexpert_ref, vfidx_ref, vlo_ref, vhi_ref, vfirst_ref, vlast_ref,
                   xs_ref, wg_ref, wu_ref, wd_ref, ys_ref, xb_ref, acc_ref, wgb_ref, wub_ref, wdb_ref):
    v = pl.program_id(0)
    f = pl.program_id(1)
    lo = vlo_ref[v]
    hi = vhi_ref[v]
    whole = (lo == 0) & (hi == TM_EXPERT)

    def swiglu_rows(rows, wg, wu, wd):
        xb = xb_ref[rows, :]
        gate = jnp.dot(xb, wg, preferred_element_type=F32)
        up = jnp.dot(xb, wu, preferred_element_type=F32)
        hid = (_silu(gate) * up).astype(BF16)
        acc_ref[rows, :] += jnp.dot(hid, wd, preferred_element_type=F32)

    @pl.when(hi > lo)
    def _():
        @pl.when(f == 0)
        def _():
            rowid = lax.broadcasted_iota(jnp.int32, xs_ref.shape, 0)
            xb_ref[...] = jnp.where((rowid >= lo) & (rowid < hi), xs_ref[...], 0.0).astype(BF16)

        @pl.when((f == 0) & (vfirst_ref[v] == 1))
        def _():
            acc_ref[...] = jnp.zeros_like(acc_ref)

        @pl.when(whole)
        def _():
            swiglu_rows(slice(None), wg_ref[0, 0].astype(BF16), wu_ref[0, 0].astype(BF16),
                        wd_ref[0, 0].astype(BF16))

        @pl.when(jnp.logical_not(whole))
        def _():
            wgb_ref[...] = wg_ref[0, 0].astype(BF16)
            wub_ref[...] = wu_ref[0, 0].astype(BF16)
            wdb_ref[...] = wd_ref[0, 0].astype(BF16)
            for sb in range(TM_EXPERT // TM_EXPERT_SUB):
                @pl.when((sb * TM_EXPERT_SUB < hi) & ((sb + 1) * TM_EXPERT_SUB > lo))
                def _():
                    swiglu_rows(slice(sb * TM_EXPERT_SUB, (sb + 1) * TM_EXPERT_SUB),
                                wgb_ref[...], wub_ref[...], wdb_ref[...])

        @pl.when((f == pl.num_programs(1) - 1) & (vlast_ref[v] == 1))
        def _():
            ys_ref[...] = acc_ref[...]


def _expert_visits(counts, n_tiles):
    n_visits = n_tiles + N_EXPERTS - 1
    ends = jnp.cumsum(counts)
    offs = ends - counts
    first_tile = offs // TM_EXPERT
    per_expert = jnp.where(counts > 0, (ends - 1) // TM_EXPERT - first_tile + 1, 0)
    vend = jnp.cumsum(per_expert)
    vstart = vend - per_expert
    total = vend[-1]
    vid = jnp.arange(n_visits, dtype=jnp.int32)
    real = vid < total
    vid_c = jnp.minimum(vid, jnp.maximum(total - 1, 0))
    vexpert = jnp.minimum(jnp.sum(vid_c[:, None] >= vend[None, :], axis=1), N_EXPERTS - 1)
    vtile = first_tile[vexpert] + vid_c - vstart[vexpert]
    lo = jnp.maximum(offs[vexpert], vtile * TM_EXPERT) - vtile * TM_EXPERT
    hi = jnp.minimum(ends[vexpert], (vtile + 1) * TM_EXPERT) - vtile * TM_EXPERT
    prev_tile = jnp.concatenate([jnp.full((1,), -1, vtile.dtype), vtile[:-1]])
    next_tile = jnp.concatenate([vtile[1:], jnp.full((1,), -1, vtile.dtype)])
    first = real & (vtile != prev_tile)
    last = real & ((vtile != next_tile) | (vid == total - 1))
    i32 = lambda a: a.astype(jnp.int32)
    fpin = jnp.where(real, -1, D_FF // TF_EXPERT - 1)
    return (i32(vtile), i32(vexpert), i32(fpin), i32(jnp.where(real, lo, 0)),
            i32(jnp.where(real, hi, 0)), i32(first), i32(last), i32(offs))


def _expert_ffn(xs, visits, w_gate, w_up, w_down, idx):
    rows_sorted = xs.shape[0]
    n_visits = visits[0].shape[0]
    nf = D_FF // TF_EXPERT

    def f_idx(v, f, vf):
        return jnp.where(vf[v] >= 0, vf[v], f)

    def rows_map(v, f, vt, ve, vf, *_):
        return (vt[v], 0)

    def wcol_map(v, f, vt, ve, vf, *_):
        return (idx, ve[v], 0, f_idx(v, f, vf))

    def wrow_map(v, f, vt, ve, vf, *_):
        return (idx, ve[v], f_idx(v, f, vf), 0)

    return pl.pallas_call(
        _expert_kernel,
        grid_spec=pltpu.PrefetchScalarGridSpec(
            num_scalar_prefetch=len(visits),
            grid=(n_visits, nf),
            in_specs=[pl.BlockSpec((TM_EXPERT, D_MODEL), rows_map),
                      pl.BlockSpec((1, 1, D_MODEL, TF_EXPERT), wcol_map),
                      pl.BlockSpec((1, 1, D_MODEL, TF_EXPERT), wcol_map),
                      pl.BlockSpec((1, 1, TF_EXPERT, D_MODEL), wrow_map)],
            out_specs=pl.BlockSpec((TM_EXPERT, D_MODEL), rows_map),
            scratch_shapes=[pltpu.VMEM((TM_EXPERT, D_MODEL), BF16),
                            pltpu.VMEM((TM_EXPERT, D_MODEL), F32),
                            pltpu.VMEM((D_MODEL, TF_EXPERT), BF16),
                            pltpu.VMEM((D_MODEL, TF_EXPERT), BF16),
                            pltpu.VMEM((TF_EXPERT, D_MODEL), BF16)]),
        out_shape=jax.ShapeDtypeStruct((rows_sorted, D_MODEL), F32),
        compiler_params=pltpu.CompilerParams(dimension_semantics=("arbitrary", "arbitrary"),
                                             vmem_limit_bytes=VMEM_LIMIT_EXPERT),
        name="moe_expert_swiglu",
    )(*visits, xs, w_gate, w_up, w_down)


def _combine_kernel(dest1_ref, dest2_ref, ys_ref, x_ref, meta_ref, g_ref, b_ref, o_ref,
                    buf_ref, sems):
    base = pl.program_id(0) * (COMBINE_SUBS * TM_GATHER)

    def gather(k, wait):
        slot = k % COMBINE_SLOTS
        if wait:
            for c in range(2):
                pltpu.make_async_copy(ys_ref.at[pl.ds(0, TM_GATHER), :], buf_ref.at[slot, c],
                                      sems.at[slot]).wait()
            return
        for j in range(TM_GATHER):
            tok = base + k * TM_GATHER + j
            _row_copy(ys_ref, dest1_ref[tok], buf_ref.at[slot, 0], j, sems.at[slot]).start(priority=0)
            _row_copy(ys_ref, dest2_ref[tok], buf_ref.at[slot, 1], j, sems.at[slot]).start(priority=1)

    for k in range(min(COMBINE_SLOTS - 1, COMBINE_SUBS)):
        gather(k, wait=False)
    for k in range(COMBINE_SUBS):
        gather(k, wait=True)
        if k + COMBINE_SLOTS - 1 < COMBINE_SUBS:
            gather(k + COMBINE_SLOTS - 1, wait=False)
        rs = slice(k * TM_GATHER, (k + 1) * TM_GATHER)
        slot = k % COMBINE_SLOTS
        meta = meta_ref[rs, :]
        moe = meta[:, R_G1:R_G1 + 1] * buf_ref[slot, 0] + meta[:, R_G2:R_G2 + 1] * buf_ref[slot, 1]
        o_ref[rs, :] = _layer_norm(ALPHA * x_ref[rs, :] + moe, g_ref[...], b_ref[...])


def _combine_ln(ys, dest1, dest2, x2d, meta, g, b):
    t = x2d.shape[0]
    tm = COMBINE_SUBS * TM_GATHER
    vec = pl.BlockSpec((1, D_MODEL), lambda i, d1, d2: (0, 0))
    return pl.pallas_call(
        _combine_kernel,
        grid_spec=pltpu.PrefetchScalarGridSpec(
            num_scalar_prefetch=2,
            grid=(t // tm,),
            in_specs=[pl.BlockSpec(memory_space=pl.ANY),
                      pl.BlockSpec((tm, D_MODEL), lambda i, d1, d2: (i, 0)),
                      pl.BlockSpec((tm, SUBLANES), lambda i, d1, d2: (i, 0)),
                      vec, vec],
            out_specs=pl.BlockSpec((tm, D_MODEL), lambda i, d1, d2: (i, 0)),
            scratch_shapes=[pltpu.VMEM((COMBINE_SLOTS, 2, TM_GATHER, D_MODEL), F32),
                            pltpu.SemaphoreType.DMA((COMBINE_SLOTS,))]),
        out_shape=jax.ShapeDtypeStruct((t, D_MODEL), F32),
        compiler_params=_cparams(("arbitrary",)),
        name="moe_combine_ln",
    )(dest1, dest2, ys, x2d, meta, g.reshape(1, D_MODEL), b.reshape(1, D_MODEL))


def _moe_ln(x2d, router_w, w_gate, w_up, w_down, idx, g, b):
    t = x2d.shape[0]
    meta, meta_t, counts = _router(x2d, router_w)
    counts = counts[0, :N_EXPERTS].astype(jnp.int32)
    *visits, group_start = _expert_visits(counts, (2 * t) // TM_EXPERT)

    def slot(expert_row, rank_row):
        expert = meta_t[expert_row].astype(jnp.int32)
        start = sum(jnp.where(expert == e, group_start[e], 0) for e in range(N_EXPERTS))
        return start + meta_t[rank_row].astype(jnp.int32)

    dest1 = slot(R_E1, R_RANK1)
    dest2 = slot(R_E2, R_RANK2)
    xs = _dispatch(x2d, dest1, dest2, 2 * t)
    ys = _expert_ffn(xs, tuple(visits), w_gate, w_up, w_down, idx)
    return _combine_ln(ys, dest1, dest2, x2d, meta, g, b)


def _pack_w_in(w):
    dn = 4 * DN_WIDTH
    ab = w[..., dn:dn + 2 * DN_HEADS]
    pad = jnp.zeros(w.shape[:-1] + (LANES - 2 * DN_HEADS,), w.dtype)
    packed = jnp.concatenate([w[..., :dn], w[..., dn + 2 * DN_HEADS:], ab, pad], axis=-1).astype(BF16)
    return packed, jnp.swapaxes(ab, -1, -2).astype(BF16)


@jax.jit
def _trunk(x, positions, w_in, conv_w, a_log, dt_bias, dn_norm_w, sinks, w_out, ln_g, ln_b,
           ffn_w_gate, ffn_w_up, ffn_w_down, router_w, moe_w_gate, moe_w_up, moe_w_down):
    batch, seq, _ = x.shape
    x2d = x.reshape(batch * seq, D_MODEL)
    cos, sin = _rope_tables(positions)
    w_packed, wabt = _pack_w_in(w_in)
    w_out_bf16 = w_out.astype(BF16)
    for layer in range(DEPTH):
        qkv, z, qsw, ktsw, vsw, ab, abt = _inproj(x2d, w_packed, wabt, layer, cos, sin)
        o_dn = _deltanet(qkv, z, ab, abt, conv_w[layer], a_log[layer], dt_bias[layer],
                         dn_norm_w[layer], batch, seq)
        o_sw = _swa(qsw, ktsw, vsw, sinks[layer], batch, seq)
        x2d = _outproj_ln(o_dn, o_sw, w_out_bf16, layer, x2d, ln_g[layer, 0], ln_b[layer, 0])
        i = layer // 2
        if layer % 2 == 0:
            x2d = _ffn_ln(x2d, ffn_w_gate, ffn_w_up, ffn_w_down, i, ln_g[layer, 1], ln_b[layer, 1])
        else:
            x2d = _moe_ln(x2d, router_w[i], moe_w_gate, moe_w_up, moe_w_down, i,
                          ln_g[layer, 1], ln_b[layer, 1])
    return x2d.reshape(batch, seq, D_MODEL)


def kernel(x, positions, w_in, conv_w, a_log, dt_bias, dn_norm_w, sinks, w_out, ln_g, ln_b,
           ffn_w_gate, ffn_w_up, ffn_w_down, router_w, moe_w_gate, moe_w_up, moe_w_down):
    return _trunk(x, positions, w_in, conv_w, a_log, dt_bias, dn_norm_w, sinks, w_out, ln_g, ln_b,
                  ffn_w_gate, ffn_w_up, ffn_w_down, router_w, moe_w_gate, moe_w_up, moe_w_down)
```

```python
import functools

import jax
import jax.numpy as jnp
from jax import lax
from jax.experimental import pallas as pl
from jax.experimental.pallas import tpu as pltpu

F32 = jnp.float32
BF16 = jnp.bfloat16

D_MODEL = 1024
DEPTH = 4
DN_HEADS = 4
DN_HEAD_DIM = 128
DN_WIDTH = DN_HEADS * DN_HEAD_DIM
CONV_WIDTH = 4
CHUNK = 64
SW_Q_HEADS = 8
SW_KV_HEADS = 2
SW_GROUP = SW_Q_HEADS // SW_KV_HEADS
SW_HEAD_DIM = 64
SW_WIDTH = SW_Q_HEADS * SW_HEAD_DIM
SW_KV_WIDTH = SW_KV_HEADS * SW_HEAD_DIM
WINDOW = 128
ROPE_THETA = 10000.0
D_FF = 3584
N_EXPERTS = 8
ALPHA = (2.0 * DEPTH) ** 0.25
LN_EPS = 1e-5
RMS_EPS = 1e-6

LANES = 128
SUBLANES = 8
DN_HALO = 16
DN_BLOCK = 512
DN_SUB = 128
DN_SCAN = 128
SWA_BLOCKS = 4
VMEM_LIMIT = 56 * 1024 * 1024

C_QKV = (0, 3 * DN_WIDTH)
C_Z = (C_QKV[1], C_QKV[1] + DN_WIDTH)
C_QSW = (C_Z[1], C_Z[1] + SW_WIDTH)
C_KVSW = (C_QSW[1], C_QSW[1] + 2 * SW_KV_WIDTH)
C_AB = (C_KVSW[1], C_KVSW[1] + LANES)
IN_COLS_PACKED = C_AB[1]

TM_PROJ = 512
TM_ROPE = 2048
TM_FFN = 1024
TF_FFN = 512
TM_ROUTE = 1024
TM_EXPERT = 1024
TF_EXPERT = 512
TM_EXPERT_SUB = 256
TM_OUT = 1024
TM_MOVE = 512
TM_GATHER = 256
COMBINE_SUBS = 8
COMBINE_SLOTS = 3


def _cparams(sem):
    return pltpu.CompilerParams(dimension_semantics=sem, vmem_limit_bytes=VMEM_LIMIT)


def _silu(x):
    return x * jax.nn.sigmoid(x)


def _softplus(x):
    return jnp.maximum(x, 0.0) + jnp.log1p(jnp.exp(-jnp.abs(x)))


def _layer_norm(y, g, b):
    mu = jnp.mean(y, axis=-1, keepdims=True)
    d = y - mu
    var = jnp.mean(d * d, axis=-1, keepdims=True)
    return d * lax.rsqrt(var + LN_EPS) * g + b


def _rope_table_kernel(pos_ref, invf_ref, sign_ref, cos_ref, sin_ref):
    ang = pos_ref[...].astype(F32) * invf_ref[...]
    cos_ref[...] = jnp.cos(ang)
    sin_ref[...] = jnp.sin(ang) * sign_ref[...]


def _rope_tables(positions):
    t = positions.size
    half = SW_HEAD_DIM // 2
    inv_freq = ROPE_THETA ** (-jnp.arange(0, SW_HEAD_DIM, 2, dtype=F32) / SW_HEAD_DIM)
    reps = LANES // half
    invf = jnp.tile(inv_freq, reps).reshape(1, LANES)
    sign = jnp.tile(jnp.concatenate([-jnp.ones((half,), F32), jnp.ones((half,), F32)]),
                    LANES // SW_HEAD_DIM).reshape(1, LANES)
    tm = min(TM_ROPE, t)
    row = pl.BlockSpec((1, LANES), lambda i: (0, 0))
    tab = pl.BlockSpec((tm, LANES), lambda i: (i, 0))
    return pl.pallas_call(
        _rope_table_kernel,
        grid=(t // tm,),
        in_specs=[pl.BlockSpec((tm, 1), lambda i: (i, 0)), row, row],
        out_specs=[tab, tab],
        out_shape=[jax.ShapeDtypeStruct((t, LANES), F32)] * 2,
        compiler_params=_cparams(("parallel",)),
        name="rope_tables",
    )(positions.reshape(t, 1), invf, sign)


def _rope(x, cos, sin_signed):
    width = x.shape[-1]
    half = SW_HEAD_DIM // 2
    lane = lax.broadcasted_iota(jnp.int32, x.shape, 1)
    first = (lane % SW_HEAD_DIM) < half
    partner = jnp.where(first, pltpu.roll(x, width - half, 1), pltpu.roll(x, half, 1))
    return x * cos + partner * sin_signed


def _inproj_kernel(x_ref, w_ref, wabt_ref, cos_ref, sin_ref,
                   qkv_ref, z_ref, qsw_ref, ktsw_ref, vsw_ref, ab_ref, abt_ref):
    xb = x_ref[...].astype(BF16)

    def mm(cols):
        return jnp.dot(xb, w_ref[0, :, cols[0]:cols[1]], preferred_element_type=F32)

    qkv_ref[...] = mm(C_QKV).astype(BF16)
    z_ref[...] = mm(C_Z).astype(BF16)
    cos = cos_ref[...]
    sin = sin_ref[...]
    reps = SW_WIDTH // LANES
    q = _rope(mm(C_QSW), jnp.tile(cos, (1, reps)), jnp.tile(sin, (1, reps))) * (SW_HEAD_DIM ** -0.5)
    qsw_ref[...] = q.astype(BF16)
    kv = mm(C_KVSW)
    ktsw_ref[...] = _rope(kv[:, :SW_KV_WIDTH], cos, sin).T.astype(BF16)
    vsw_ref[...] = kv[:, SW_KV_WIDTH:].astype(BF16)
    ab_ref[...] = mm(C_AB)
    abt_ref[...] = lax.dot_general(wabt_ref[0], xb, (((1,), (1,)), ((), ())),
                                   preferred_element_type=F32)


def _inproj(x2d, w_packed, wabt, layer, cos, sin):
    t = x2d.shape[0]
    tm = min(TM_PROJ, t)

    def rows(width):
        return pl.BlockSpec((tm, width), lambda i: (i, 0))

    return pl.pallas_call(
        _inproj_kernel,
        grid=(t // tm,),
        in_specs=[rows(D_MODEL),
                  pl.BlockSpec((1, D_MODEL, IN_COLS_PACKED), lambda i: (layer, 0, 0)),
                  pl.BlockSpec((1, SUBLANES, D_MODEL), lambda i: (layer, 0, 0)),
                  rows(LANES), rows(LANES)],
        out_specs=[rows(3 * DN_WIDTH), rows(DN_WIDTH), rows(SW_WIDTH),
                   pl.BlockSpec((SW_KV_WIDTH, tm), lambda i: (0, i)), rows(SW_KV_WIDTH),
                   rows(LANES), pl.BlockSpec((SUBLANES, tm), lambda i: (0, i))],
        out_shape=[jax.ShapeDtypeStruct((t, 3 * DN_WIDTH), BF16),
                   jax.ShapeDtypeStruct((t, DN_WIDTH), BF16),
                   jax.ShapeDtypeStruct((t, SW_WIDTH), BF16),
                   jax.ShapeDtypeStruct((SW_KV_WIDTH, t), BF16),
                   jax.ShapeDtypeStruct((t, SW_KV_WIDTH), BF16),
                   jax.ShapeDtypeStruct((t, LANES), F32),
                   jax.ShapeDtypeStruct((SUBLANES, t), F32)],
        compiler_params=_cparams(("parallel",)),
        name="in_proj",
    )(x2d, w_packed, wabt, cos, sin)


def _bmm(a, b):
    return lax.dot_general(a, b, (((2,), (1,)), ((0,), (0,))), preferred_element_type=F32)


def _bmm_nt(a, b):
    return lax.dot_general(a, b, (((2,), (2,)), ((0,), (0,))), preferred_element_type=F32)


def _unit_lower_inverse(lmat, row, col):
    same16 = ((row // 16) == (col // 16))[None]
    same32 = ((row // 32) == (col // 32))[None]
    eye = (row == col).astype(F32)[None]
    ld = jnp.where(same16, lmat, 0.0)
    inv = eye - ld
    power = ld.astype(BF16)
    for _ in range(3):
        power = _bmm(power, power).astype(BF16)
        inv = inv + _bmm(inv.astype(BF16), power)
    lmat_bf16 = lmat.astype(BF16)
    for off_diag in (same32 & ~same16, ~same32):
        c = jnp.where(off_diag, lmat_bf16, jnp.zeros_like(lmat_bf16))
        inv_bf16 = inv.astype(BF16)
        inv = inv - _bmm(_bmm(inv_bf16, c).astype(BF16), inv_bf16)
    return inv


def _dn_prep_kernel(qkv_ref, halo_ref, ab_ref, abt_ref, convw_ref, prow_ref, pcol_ref,
                    u_ref, w_ref, qd_ref, attn_ref, ket_ref, gl_ref, xp_ref, *, blocks_per_seq):
    tb = qkv_ref.shape[0]
    first = (pl.program_id(0) % blocks_per_seq) == 0

    halo = halo_ref[...]
    xp_ref[0:DN_HALO, :] = jnp.where(first, jnp.zeros_like(halo), halo)
    xp_ref[DN_HALO:DN_HALO + tb, :] = qkv_ref[...]
    convw = convw_ref[...]
    span = DN_HALO + DN_SUB
    out_row = lax.broadcasted_iota(jnp.int32, ((CONV_WIDTH - 1) * DN_SUB, span), 0)
    src_row = lax.broadcasted_iota(jnp.int32, ((CONV_WIDTH - 1) * DN_SUB, span), 1)
    back = (CONV_WIDTH - 1) - out_row // DN_SUB
    select = (src_row == DN_HALO + out_row % DN_SUB - back).astype(BF16)
    tiles = []
    for r in range(tb // DN_SUB):
        window = xp_ref[r * DN_SUB:r * DN_SUB + span, :]
        shifted = jnp.dot(select, window, preferred_element_type=F32)
        y = window[DN_HALO:, :].astype(F32) * convw[CONV_WIDTH - 1:CONV_WIDTH, :]
        for tap in range(CONV_WIDTH - 1):
            y = y + shifted[tap * DN_SUB:(tap + 1) * DN_SUB, :] * convw[tap:tap + 1, :]
        tiles.append(y)
    qkv = _silu(jnp.concatenate(tiles, axis=0))

    row = lax.broadcasted_iota(jnp.int32, (DN_SUB, DN_SUB), 0)
    col = lax.broadcasted_iota(jnp.int32, (DN_SUB, DN_SUB), 1)
    same_chunk = (row // CHUNK) == (col // CHUNK)
    lower_incl = same_chunk & (row >= col)
    strict_lower = same_chunk & (row > col)
    cum_lhs = jnp.concatenate([lower_incl.astype(F32), same_chunk.astype(F32)], axis=0)
    cum_rhs_t = (same_chunk & (row <= col)).astype(F32)

    ab = ab_ref[...]
    g_cols = -jnp.exp(prow_ref[0:1, :]) * _softplus(ab + prow_ref[1:2, :])
    g_rows = -jnp.exp(pcol_ref[:, 0:1]) * _softplus(abt_ref[...] + pcol_ref[:, 1:2])
    beta_cols = jax.nn.sigmoid(ab)

    nsub = tb // DN_SUB
    subs = [slice(s * DN_SUB, (s + 1) * DN_SUB) for s in range(nsub)]
    gl_rows, gcum_cols, gtot_cols, gcum_rows = [], [], [], []
    for rs in subs:
        cum = jnp.dot(cum_lhs, g_cols[rs, :], preferred_element_type=F32,
                      precision=lax.Precision.HIGHEST)
        gcum_cols.append(cum[:DN_SUB])
        gtot_cols.append(cum[DN_SUB:])
        gcum_rows.append(jnp.dot(g_rows[:, rs], cum_rhs_t, preferred_element_type=F32,
                                 precision=lax.Precision.HIGHEST))
        for c in range(DN_SUB // CHUNK):
            tot = jnp.exp(cum[DN_SUB + c * CHUNK:DN_SUB + c * CHUNK + 1, :])
            gl_rows.append(jnp.concatenate(
                [jnp.broadcast_to(tot[:, h:h + 1], (1, DN_HEAD_DIM)) for h in range(DN_HEADS)], axis=1))
    gl_ref[...] = jnp.concatenate(gl_rows, axis=0)

    chains = [(s, h) for s in range(nsub) for h in range(DN_HEADS)]

    def stack(fn):
        return jnp.stack([fn(s, h) for s, h in chains], axis=0)

    def head_cols(base):
        return stack(lambda s, h: qkv[subs[s], base + h * DN_HEAD_DIM:base + (h + 1) * DN_HEAD_DIM])

    q = head_cols(0)
    k = head_cols(DN_WIDTH)
    v = head_cols(2 * DN_WIDTH)
    gc = stack(lambda s, h: gcum_cols[s][:, h:h + 1])
    gt = stack(lambda s, h: gtot_cols[s][:, h:h + 1])
    gr = stack(lambda s, h: gcum_rows[s][h:h + 1, :])
    beta = stack(lambda s, h: beta_cols[subs[s], DN_HEADS + h:DN_HEADS + h + 1])

    q = q * lax.rsqrt(jnp.sum(q * q, axis=-1, keepdims=True) + 1e-6) * (DN_HEAD_DIM ** -0.5)
    k = k * lax.rsqrt(jnp.sum(k * k, axis=-1, keepdims=True) + 1e-6)
    lower3 = lower_incl[None]
    decay = jnp.where(lower3, jnp.exp(jnp.where(lower3, gc - gr, 0.0)), 0.0)
    k_beta = k * beta
    gram = _bmm_nt(jnp.concatenate([q, k_beta], axis=1).astype(BF16), k.astype(BF16))
    attn = jnp.where(lower3, gram[:, :DN_SUB] * decay, 0.0)
    lmat = jnp.where(strict_lower[None], gram[:, DN_SUB:] * decay, 0.0)
    tinv = _unit_lower_inverse(lmat, row, col)
    exp_gc = jnp.exp(gc)
    uw = _bmm(tinv.astype(BF16), jnp.concatenate([v * beta, k_beta * exp_gc], axis=2).astype(BF16))
    q_dec = (q * exp_gc).astype(BF16)
    k_end = k * jnp.exp(gt - gc)
    compact = attn[:, :, 0:CHUNK]
    for c in range(1, DN_SUB // CHUNK):
        compact = compact + attn[:, :, c * CHUNK:(c + 1) * CHUNK]
    compact = compact.astype(BF16)
    uw = uw.astype(BF16)

    for idx, (s, h) in enumerate(chains):
        rs = subs[s]
        hs = slice(h * DN_HEAD_DIM, (h + 1) * DN_HEAD_DIM)
        u_ref[rs, hs] = uw[idx, :, :DN_HEAD_DIM]
        w_ref[rs, hs] = uw[idx, :, DN_HEAD_DIM:]
        qd_ref[rs, hs] = q_dec[idx]
        ket_ref[0, hs, rs] = k_end[idx].T.astype(BF16)
        attn_ref[rs, h * CHUNK:(h + 1) * CHUNK] = compact[idx]


def _dn_scan_kernel(u_ref, w_ref, qd_ref, attn_ref, ket_ref, z_ref, gl_ref, normw_ref,
                    o_ref, state_ref):
    step = pl.program_id(0)

    @pl.when(step == 0)
    def _():
        state_ref[...] = jnp.zeros_like(state_ref)

    batch = u_ref.shape[0]
    normw = normw_ref[...]
    chains = [(b, h, slice(h * DN_HEAD_DIM, (h + 1) * DN_HEAD_DIM))
              for b in range(batch) for h in range(DN_HEADS)]
    states = [state_ref[b * DN_HEADS + h] for b, h, _ in chains]
    for c in range(DN_SCAN // CHUNK):
        rs = slice(c * CHUNK, (c + 1) * CHUNK)
        ws_qs = [jnp.dot(jnp.concatenate([w_ref[b, rs, hs], qd_ref[b, rs, hs]], axis=0),
                         state.astype(BF16), preferred_element_type=F32)
                 for (b, h, hs), state in zip(chains, states)]
        v_new = [(u_ref[b, rs, hs].astype(F32) - r[:CHUNK]).astype(BF16)
                 for (b, h, hs), r in zip(chains, ws_qs)]
        av_kv = [jnp.dot(jnp.concatenate([attn_ref[b, rs, h * CHUNK:(h + 1) * CHUNK],
                                          ket_ref[b, hs, rs]], axis=0),
                         vn, preferred_element_type=F32)
                 for (b, h, hs), vn in zip(chains, v_new)]
        states = [state * gl_ref[b, 0, c:c + 1, hs] + r[CHUNK:]
                  for (b, h, hs), state, r in zip(chains, states, av_kv)]
        for (b, h, hs), r1, r2 in zip(chains, ws_qs, av_kv):
            o = r1[CHUNK:] + r2[:CHUNK]
            o = o * lax.rsqrt(jnp.mean(o * o, axis=-1, keepdims=True) + RMS_EPS)
            o_ref[b, rs, hs] = (o * normw * _silu(z_ref[b, rs, hs].astype(F32))).astype(o_ref.dtype)
    for (b, h, _), state in zip(chains, states):
        state_ref[b * DN_HEADS + h] = state


def _deltanet(qkv, z, ab, abt, conv_w, a_log, dt_bias, dn_norm_w, batch, seq):
    t = batch * seq
    tb = min(DN_BLOCK, seq)
    blocks_per_seq = seq // tb
    prow = jnp.zeros((SUBLANES, LANES), F32).at[0, :DN_HEADS].set(a_log).at[1, :DN_HEADS].set(dt_bias)
    pcol = (jnp.zeros((SUBLANES, LANES), F32)
            .at[:DN_HEADS, 0].set(a_log).at[DN_HEADS:2 * DN_HEADS, 0].set(a_log)
            .at[:DN_HEADS, 1].set(dt_bias).at[DN_HEADS:2 * DN_HEADS, 1].set(dt_bias))

    def rows(width):
        return pl.BlockSpec((tb, width), lambda i: (i, 0))

    def const(shape):
        return pl.BlockSpec(shape, lambda i: (0,) * len(shape))

    u, w, qd, attn, ket, gl = pl.pallas_call(
        functools.partial(_dn_prep_kernel, blocks_per_seq=blocks_per_seq),
        grid=(t // tb,),
        in_specs=[rows(3 * DN_WIDTH),
                  pl.BlockSpec((DN_HALO, 3 * DN_WIDTH),
                               lambda i: (jnp.maximum(i * (tb // DN_HALO) - 1, 0), 0)),
                  rows(LANES), pl.BlockSpec((SUBLANES, tb), lambda i: (0, i)),
                  const((CONV_WIDTH, 3 * DN_WIDTH)), const((SUBLANES, LANES)), const((SUBLANES, LANES))],
        out_specs=[rows(DN_WIDTH), rows(DN_WIDTH), rows(DN_WIDTH), rows(DN_HEADS * CHUNK),
                   pl.BlockSpec((1, DN_WIDTH, tb), lambda i: (i // blocks_per_seq, 0, i % blocks_per_seq)),
                   pl.BlockSpec((tb // CHUNK, DN_WIDTH), lambda i: (i, 0))],
        out_shape=[jax.ShapeDtypeStruct((t, DN_WIDTH), BF16)] * 3
        + [jax.ShapeDtypeStruct((t, DN_HEADS * CHUNK), BF16),
           jax.ShapeDtypeStruct((batch, DN_WIDTH, seq), BF16),
           jax.ShapeDtypeStruct((t // CHUNK, DN_WIDTH), F32)],
        scratch_shapes=[pltpu.VMEM((DN_HALO + tb, 3 * DN_WIDTH), BF16)],
        compiler_params=_cparams(("parallel",)),
        name="deltanet_prep",
    )(qkv, qkv, ab, abt, conv_w, prow, pcol)

    def seq_rows(width):
        return pl.BlockSpec((batch, DN_SCAN, width), lambda c: (0, c, 0))

    def view(a):
        return a.reshape(batch, seq, a.shape[-1])

    o = pl.pallas_call(
        _dn_scan_kernel,
        grid=(seq // DN_SCAN,),
        in_specs=[seq_rows(DN_WIDTH), seq_rows(DN_WIDTH), seq_rows(DN_WIDTH), seq_rows(DN_HEADS * CHUNK),
                  pl.BlockSpec((batch, DN_WIDTH, DN_SCAN), lambda c: (0, 0, c)),
                  seq_rows(DN_WIDTH),
                  pl.BlockSpec((batch, 1, DN_SCAN // CHUNK, DN_WIDTH), lambda c: (0, c, 0, 0)),
                  pl.BlockSpec((1, DN_HEAD_DIM), lambda c: (0, 0))],
        out_specs=seq_rows(DN_WIDTH),
        out_shape=jax.ShapeDtypeStruct((batch, seq, DN_WIDTH), BF16),
        scratch_shapes=[pltpu.VMEM((batch * DN_HEADS, DN_HEAD_DIM, DN_HEAD_DIM), F32)],
        compiler_params=_cparams(("arbitrary",)),
        name="deltanet_scan",
    )(view(u), view(w), view(qd), view(attn), ket, view(z),
      gl.reshape(batch, seq // DN_SCAN, DN_SCAN // CHUNK, DN_WIDTH), dn_norm_w.reshape(1, DN_HEAD_DIM))
    return o.reshape(t, DN_WIDTH)


def _swa_kernel(q_ref, ktc_ref, ktp_ref, vc_ref, vp_ref, sink_ref, o_ref):
    qi = lax.broadcasted_iota(jnp.int32, (WINDOW, WINDOW), 0)
    kj = lax.broadcasted_iota(jnp.int32, (WINDOW, WINDOW), 1)
    from_prev = kj > qi
    no_prev = from_prev & (pl.program_id(1) == 0)
    kt = jnp.concatenate([ktp_ref[...], ktc_ref[...]], axis=1)
    v = jnp.concatenate([vp_ref[...], vc_ref[...]], axis=0)
    scores = []
    for j in range(SWA_BLOCKS):
        for hq in range(SW_Q_HEADS):
            hk = hq // SW_GROUP
            q = q_ref[j * WINDOW:(j + 1) * WINDOW, hq * SW_HEAD_DIM:(hq + 1) * SW_HEAD_DIM]
            s = jnp.dot(q, kt[hk * SW_HEAD_DIM:(hk + 1) * SW_HEAD_DIM, j * WINDOW:(j + 2) * WINDOW],
                        preferred_element_type=F32)
            s = jnp.where(from_prev, s[:, :WINDOW], s[:, WINDOW:])
            scores.append(jnp.where(no_prev, -jnp.inf, s) if j == 0 else s)
    s = jnp.concatenate(scores, axis=0)
    sink = sink_ref[...]
    m = jnp.maximum(jnp.max(s, axis=-1, keepdims=True), sink)
    p = jnp.exp(s - m).astype(BF16)
    sink_term = jnp.exp(sink - m)
    zero = jnp.zeros((WINDOW, WINDOW), BF16)
    ones = jnp.ones((WINDOW, SW_HEAD_DIM), BF16)
    v_ones = [[jnp.concatenate([v[jb * WINDOW:(jb + 1) * WINDOW, hk * SW_HEAD_DIM:(hk + 1) * SW_HEAD_DIM],
                                ones], axis=1) for hk in range(SW_KV_HEADS)]
              for jb in range(SWA_BLOCKS + 1)]
    pvs = []
    for j in range(SWA_BLOCKS):
        for hq in range(SW_Q_HEADS):
            hk = hq // SW_GROUP
            rs = slice((j * SW_Q_HEADS + hq) * WINDOW, (j * SW_Q_HEADS + hq + 1) * WINDOW)
            p_prev = jnp.where(from_prev, p[rs], zero)
            p_cur = jnp.where(from_prev, zero, p[rs])
            pvs.append(jnp.dot(p_prev, v_ones[j][hk], preferred_element_type=F32)
                       + jnp.dot(p_cur, v_ones[j + 1][hk], preferred_element_type=F32))
    pv = jnp.concatenate(pvs, axis=0)
    out = pv[:, :SW_HEAD_DIM] / (pv[:, SW_HEAD_DIM:SW_HEAD_DIM + 1] + sink_term)
    for j in range(SWA_BLOCKS):
        heads = [out[(j * SW_Q_HEADS + hq) * WINDOW:(j * SW_Q_HEADS + hq + 1) * WINDOW]
                 for hq in range(SW_Q_HEADS)]
        o_ref[j * WINDOW:(j + 1) * WINDOW, :] = jnp.concatenate(heads, axis=1).astype(o_ref.dtype)


def _swa(qsw, ktsw, vsw, sinks, batch, seq):
    t = batch * seq
    tq = SWA_BLOCKS * WINDOW
    nstep = seq // tq
    sink_rows = jnp.tile(jnp.repeat(sinks.astype(F32), WINDOW), SWA_BLOCKS).reshape(-1, 1)

    def cur(b, i):
        return b * nstep + i

    def prev(b, i):
        return (b * nstep + i) * SWA_BLOCKS - jnp.minimum(i, 1)

    return pl.pallas_call(
        _swa_kernel,
        grid=(batch, nstep),
        in_specs=[pl.BlockSpec((tq, SW_WIDTH), lambda b, i: (cur(b, i), 0)),
                  pl.BlockSpec((SW_KV_WIDTH, tq), lambda b, i: (0, cur(b, i))),
                  pl.BlockSpec((SW_KV_WIDTH, WINDOW), lambda b, i: (0, prev(b, i))),
                  pl.BlockSpec((tq, SW_KV_WIDTH), lambda b, i: (cur(b, i), 0)),
                  pl.BlockSpec((WINDOW, SW_KV_WIDTH), lambda b, i: (prev(b, i), 0)),
                  pl.BlockSpec((SWA_BLOCKS * SW_Q_HEADS * WINDOW, 1), lambda b, i: (0, 0))],
        out_specs=pl.BlockSpec((tq, SW_WIDTH), lambda b, i: (cur(b, i), 0)),
        out_shape=jax.ShapeDtypeStruct((t, SW_WIDTH), BF16),
        compiler_params=_cparams(("parallel", "parallel")),
        name="sliding_window_attn",
    )(qsw, ktsw, ktsw, vsw, vsw, sink_rows)


def _outproj_kernel(odn_ref, osw_ref, w_ref, x_ref, g_ref, b_ref, o_ref):
    mix = (jnp.dot(odn_ref[...], w_ref[0, 0:DN_WIDTH, :], preferred_element_type=F32)
           + jnp.dot(osw_ref[...], w_ref[0, DN_WIDTH:, :], preferred_element_type=F32))
    o_ref[...] = _layer_norm(ALPHA * x_ref[...] + mix, g_ref[...], b_ref[...])


def _outproj_ln(o_dn, o_sw, w_out_bf16, layer, x2d, g, b):
    t = x2d.shape[0]
    tm = min(TM_OUT, t)

    def rows(width):
        return pl.BlockSpec((tm, width), lambda i: (i, 0))

    vec = pl.BlockSpec((1, D_MODEL), lambda i: (0, 0))
    return pl.pallas_call(
        _outproj_kernel,
        grid=(t // tm,),
        in_specs=[rows(DN_WIDTH), rows(SW_WIDTH),
                  pl.BlockSpec((1, DN_WIDTH + SW_WIDTH, D_MODEL), lambda i: (layer, 0, 0)),
                  rows(D_MODEL), vec, vec],
        out_specs=rows(D_MODEL),
        out_shape=jax.ShapeDtypeStruct((t, D_MODEL), F32),
        compiler_params=_cparams(("parallel",)),
        name="out_proj_ln",
    )(o_dn, o_sw, w_out_bf16, x2d, g.reshape(1, D_MODEL), b.reshape(1, D_MODEL))


def _ffn_kernel(x_ref, wg_ref, wu_ref, wd_ref, g_ref, b_ref, o_ref, xb_ref, acc_ref):
    f = pl.program_id(1)

    @pl.when(f == 0)
    def _():
        xb_ref[...] = x_ref[...].astype(BF16)
        acc_ref[...] = jnp.zeros_like(acc_ref)

    xb = xb_ref[...]
    gate = jnp.dot(xb, wg_ref[0].astype(BF16), preferred_element_type=F32)
    up = jnp.dot(xb, wu_ref[0].astype(BF16), preferred_element_type=F32)
    hid = (_silu(gate) * up).astype(BF16)
    acc_ref[...] += jnp.dot(hid, wd_ref[0].astype(BF16), preferred_element_type=F32)

    @pl.when(f == pl.num_programs(1) - 1)
    def _():
        o_ref[...] = _layer_norm(ALPHA * x_ref[...] + acc_ref[...], g_ref[...], b_ref[...])


def _ffn_ln(x2d, w_gate, w_up, w_down, idx, g, b):
    t = x2d.shape[0]
    tm = min(TM_FFN, t)
    vec = pl.BlockSpec((1, D_MODEL), lambda i, f: (0, 0))
    return pl.pallas_call(
        _ffn_kernel,
        grid=(t // tm, D_FF // TF_FFN),
        in_specs=[pl.BlockSpec((tm, D_MODEL), lambda i, f: (i, 0)),
                  pl.BlockSpec((1, D_MODEL, TF_FFN), lambda i, f: (idx, 0, f)),
                  pl.BlockSpec((1, D_MODEL, TF_FFN), lambda i, f: (idx, 0, f)),
                  pl.BlockSpec((1, TF_FFN, D_MODEL), lambda i, f: (idx, f, 0)),
                  vec, vec],
        out_specs=pl.BlockSpec((tm, D_MODEL), lambda i, f: (i, 0)),
        out_shape=jax.ShapeDtypeStruct((t, D_MODEL), F32),
        scratch_shapes=[pltpu.VMEM((tm, D_MODEL), BF16), pltpu.VMEM((tm, D_MODEL), F32)],
        compiler_params=_cparams(("parallel", "arbitrary")),
        name="dense_swiglu_ln",
    )(x2d, w_gate, w_up, w_down, g.reshape(1, D_MODEL), b.reshape(1, D_MODEL))


R_E1, R_E2, R_RANK1, R_RANK2, R_G1, R_G2 = range(6)


def _router_kernel(x_ref, wr_ref, meta_ref, metat_ref, count_ref, carry_ref):
    i = pl.program_id(0)

    @pl.when(i == 0)
    def _():
        carry_ref[...] = jnp.zeros_like(carry_ref)

    tm = x_ref.shape[0]
    logits = jnp.dot(x_ref[...], wr_ref[...], preferred_element_type=F32,
                     precision=lax.Precision.HIGHEST)
    lane = lax.broadcasted_iota(jnp.int32, logits.shape, 1)
    logits = jnp.where(lane < N_EXPERTS, logits, -jnp.inf)
    m1 = jnp.max(logits, axis=-1, keepdims=True)
    e1 = jnp.min(jnp.where(logits == m1, lane, LANES), axis=-1, keepdims=True)
    rest = jnp.where(lane == e1, -jnp.inf, logits)
    m2 = jnp.max(rest, axis=-1, keepdims=True)
    e2 = jnp.min(jnp.where(rest == m2, lane, LANES), axis=-1, keepdims=True)
    ex = jnp.exp(m2 - m1)
    g1 = 1.0 / (1.0 + ex)
    g2 = ex / (1.0 + ex)

    hit = ((lane == e1) | (lane == e2)).astype(BF16)
    r = lax.broadcasted_iota(jnp.int32, (tm, tm), 0)
    c = lax.broadcasted_iota(jnp.int32, (tm, tm), 1)
    before = jnp.dot((c < r).astype(BF16), hit, preferred_element_type=F32) + carry_ref[0:1, :]
    rank1 = jnp.sum(jnp.where(lane == e1, before, 0.0), axis=-1, keepdims=True)
    rank2 = jnp.sum(jnp.where(lane == e2, before, 0.0), axis=-1, keepdims=True)
    carry_ref[0:1, :] = carry_ref[0:1, :] + jnp.sum(hit.astype(F32), axis=0, keepdims=True)
    count_ref[...] = jnp.broadcast_to(carry_ref[0:1, :], count_ref.shape)

    meta = jnp.zeros((tm, LANES), F32)
    for idx, val in ((R_E1, e1.astype(F32)), (R_E2, e2.astype(F32)), (R_RANK1, rank1),
                     (R_RANK2, rank2), (R_G1, g1), (R_G2, g2)):
        meta = jnp.where(lane == idx, val, meta)
    meta_ref[...] = meta[:, :SUBLANES]
    metat_ref[...] = meta.T[:SUBLANES, :]


def _router(x2d, router_w):
    t = x2d.shape[0]
    tm = min(TM_ROUTE, t)
    wr = jnp.zeros((D_MODEL, LANES), F32).at[:, :N_EXPERTS].set(router_w)
    return pl.pallas_call(
        _router_kernel,
        grid=(t // tm,),
        in_specs=[pl.BlockSpec((tm, D_MODEL), lambda i: (i, 0)),
                  pl.BlockSpec((D_MODEL, LANES), lambda i: (0, 0))],
        out_specs=[pl.BlockSpec((tm, SUBLANES), lambda i: (i, 0)),
                   pl.BlockSpec((SUBLANES, tm), lambda i: (0, i)),
                   pl.BlockSpec((SUBLANES, LANES), lambda i: (0, 0))],
        out_shape=[jax.ShapeDtypeStruct((t, SUBLANES), F32),
                   jax.ShapeDtypeStruct((SUBLANES, t), F32),
                   jax.ShapeDtypeStruct((SUBLANES, LANES), F32)],
        scratch_shapes=[pltpu.VMEM((SUBLANES, LANES), F32)],
        compiler_params=_cparams(("arbitrary",)),
        name="moe_router",
    )(x2d, wr)


def _row_copy(src_ref, src_row, dst_ref, dst_row, sem):
    return pltpu.make_async_copy(src_ref.at[pl.ds(src_row, 1), :], dst_ref.at[pl.ds(dst_row, 1), :], sem)


def _dispatch_kernel(dest1_ref, dest2_ref, x_ref, xs_ref, sem):
    base = pl.program_id(0) * TM_MOVE

    for j in range(TM_MOVE):
        _row_copy(x_ref, j, xs_ref, dest1_ref[base + j], sem).start(priority=0)
        _row_copy(x_ref, j, xs_ref, dest2_ref[base + j], sem).start(priority=1)
    for _ in range(2):
        pltpu.make_async_copy(x_ref, xs_ref.at[pl.ds(0, TM_MOVE), :], sem).wait()


def _dispatch(x2d, dest1, dest2, rows_sorted):
    t = x2d.shape[0]
    return pl.pallas_call(
        _dispatch_kernel,
        grid_spec=pltpu.PrefetchScalarGridSpec(
            num_scalar_prefetch=2,
            grid=(t // TM_MOVE,),
            in_specs=[pl.BlockSpec((TM_MOVE, D_MODEL), lambda i, d1, d2: (i, 0))],
            out_specs=pl.BlockSpec(memory_space=pl.ANY),
            scratch_shapes=[pltpu.SemaphoreType.DMA(())]),
        out_shape=jax.ShapeDtypeStruct((rows_sorted, D_MODEL), F32),
        compiler_params=_cparams(("arbitrary",)),
        name="moe_dispatch",
    )(dest1, dest2, x2d)


def _expert_kernel(vtile_ref, vexpert_ref, vfidx_ref, vlo_ref, vhi_ref, vfirst_ref, vlast_ref,
                   xs_ref, wg_ref, wu_ref, wd_ref, ys_ref, xb_ref, acc_ref, wgb_ref, wub_ref, wdb_ref):
    v = pl.program_id(0)
    f = pl.program_id(1)
    lo = vlo_ref[v]
    hi = vhi_ref[v]
    whole = (lo == 0) & (hi == TM_EXPERT)

    def swiglu_rows(rows, wg, wu, wd):
        xb = xb_ref[rows, :]
        gate = jnp.dot(xb, wg, preferred_element_type=F32)
        up = jnp.dot(xb, wu, preferred_element_type=F32)
        hid = (_silu(gate) * up).astype(BF16)
        acc_ref[rows, :] += jnp.dot(hid, wd, preferred_element_type=F32)

    @pl.when(hi > lo)
    def _():
        @pl.when(f == 0)
        def _():
            rowid = lax.broadcasted_iota(jnp.int32, xs_ref.shape, 0)
            xb_ref[...] = jnp.where((rowid >= lo) & (rowid < hi), xs_ref[...], 0.0).astype(BF16)

        @pl.when((f == 0) & (vfirst_ref[v] == 1))
        def _():
            acc_ref[...] = jnp.zeros_like(acc_ref)

        @pl.when(whole)
        def _():
            swiglu_rows(slice(None), wg_ref[0, 0].astype(BF16), wu_ref[0, 0].astype(BF16),
                        wd_ref[0, 0].astype(BF16))

        @pl.when(jnp.logical_not(whole))
        def _():
            wgb_ref[...] = wg_ref[0, 0].astype(BF16)
            wub_ref[...] = wu_ref[0, 0].astype(BF16)
            wdb_ref[...] = wd_ref[0, 0].astype(BF16)
            for sb in range(TM_EXPERT // TM_EXPERT_SUB):
                @pl.when((sb * TM_EXPERT_SUB < hi) & ((sb + 1) * TM_EXPERT_SUB > lo))
                def _():
                    swiglu_rows(slice(sb * TM_EXPERT_SUB, (sb + 1) * TM_EXPERT_SUB),
                                wgb_ref[...], wub_ref[...], wdb_ref[...])

        @pl.when((f == pl.num_programs(1) - 1) & (vlast_ref[v] == 1))
        def _():
            ys_ref[...] = acc_ref[...]


def _expert_visits(counts, n_tiles):
    n_visits = n_tiles + N_EXPERTS - 1
    ends = jnp.cumsum(counts)
    offs = ends - counts
    first_tile = offs // TM_EXPERT
    per_expert = jnp.where(counts > 0, (ends - 1) // TM_EXPERT - first_tile + 1, 0)
    vend = jnp.cumsum(per_expert)
    vstart = vend - per_expert
    total = vend[-1]
    vid = jnp.arange(n_visits, dtype=jnp.int32)
    real = vid < total
    vid_c = jnp.minimum(vid, jnp.maximum(total - 1, 0))
    vexpert = jnp.minimum(jnp.sum(vid_c[:, None] >= vend[None, :], axis=1), N_EXPERTS - 1)
    vtile = first_tile[vexpert] + vid_c - vstart[vexpert]
    lo = jnp.maximum(offs[vexpert], vtile * TM_EXPERT) - vtile * TM_EXPERT
    hi = jnp.minimum(ends[vexpert], (vtile + 1) * TM_EXPERT) - vtile * TM_EXPERT
    prev_tile = jnp.concatenate([jnp.full((1,), -1, vtile.dtype), vtile[:-1]])
    next_tile = jnp.concatenate([vtile[1:], jnp.full((1,), -1, vtile.dtype)])
    first = real & (vtile != prev_tile)
    last = real & ((vtile != next_tile) | (vid == total - 1))
    i32 = lambda a: a.astype(jnp.int32)
    fpin = jnp.where(real, -1, D_FF // TF_EXPERT - 1)
    return (i32(vtile), i32(vexpert), i32(fpin), i32(jnp.where(real, lo, 0)),
            i32(jnp.where(real, hi, 0)), i32(first), i32(last), i32(offs))


def _expert_ffn(xs, visits, w_gate, w_up, w_down, idx):
    rows_sorted = xs.shape[0]
    n_visits = visits[0].shape[0]
    nf = D_FF // TF_EXPERT

    def f_idx(v, f, vf):
        return jnp.where(vf[v] >= 0, vf[v], f)

    def rows_map(v, f, vt, ve, vf, *_):
        return (vt[v], 0)

    def wcol_map(v, f, vt, ve, vf, *_):
        return (idx, ve[v], 0, f_idx(v, f, vf))

    def wrow_map(v, f, vt, ve, vf, *_):
        return (idx, ve[v], f_idx(v, f, vf), 0)

    return pl.pallas_call(
        _expert_kernel,
        grid_spec=pltpu.PrefetchScalarGridSpec(
            num_scalar_prefetch=len(visits),
            grid=(n_visits, nf),
            in_specs=[pl.BlockSpec((TM_EXPERT, D_MODEL), rows_map),
                      pl.BlockSpec((1, 1, D_MODEL, TF_EXPERT), wcol_map),
                      pl.BlockSpec((1, 1, D_MODEL, TF_EXPERT), wcol_map),
                      pl.BlockSpec((1, 1, TF_EXPERT, D_MODEL), wrow_map)],
            out_specs=pl.BlockSpec((TM_EXPERT, D_MODEL), rows_map),
            scratch_shapes=[pltpu.VMEM((TM_EXPERT, D_MODEL), BF16),
                            pltpu.VMEM((TM_EXPERT, D_MODEL), F32),
                            pltpu.VMEM((D_MODEL, TF_EXPERT), BF16),
                            pltpu.VMEM((D_MODEL, TF_EXPERT), BF16),
                            pltpu.VMEM((TF_EXPERT, D_MODEL), BF16)]),
        out_shape=jax.ShapeDtypeStruct((rows_sorted, D_MODEL), F32),
        compiler_params=_cparams(("arbitrary", "arbitrary")),
        name="moe_expert_swiglu",
    )(*visits, xs, w_gate, w_up, w_down)


def _combine_kernel(dest1_ref, dest2_ref, ys_ref, x_ref, meta_ref, g_ref, b_ref, o_ref,
                    buf_ref, sems):
    base = pl.program_id(0) * (COMBINE_SUBS * TM_GATHER)

    def gather(k, wait):
        slot = k % COMBINE_SLOTS
        if wait:
            for c in range(2):
                pltpu.make_async_copy(ys_ref.at[pl.ds(0, TM_GATHER), :], buf_ref.at[slot, c],
                                      sems.at[slot]).wait()
            return
        for j in range(TM_GATHER):
            tok = base + k * TM_GATHER + j
            _row_copy(ys_ref, dest1_ref[tok], buf_ref.at[slot, 0], j, sems.at[slot]).start(priority=0)
            _row_copy(ys_ref, dest2_ref[tok], buf_ref.at[slot, 1], j, sems.at[slot]).start(priority=1)

    for k in range(min(COMBINE_SLOTS - 1, COMBINE_SUBS)):
        gather(k, wait=False)
    for k in range(COMBINE_SUBS):
        gather(k, wait=True)
        if k + COMBINE_SLOTS - 1 < COMBINE_SUBS:
            gather(k + COMBINE_SLOTS - 1, wait=False)
        rs = slice(k * TM_GATHER, (k + 1) * TM_GATHER)
        slot = k % COMBINE_SLOTS
        meta = meta_ref[rs, :]
        moe = meta[:, R_G1:R_G1 + 1] * buf_ref[slot, 0] + meta[:, R_G2:R_G2 + 1] * buf_ref[slot, 1]
        o_ref[rs, :] = _layer_norm(ALPHA * x_ref[rs, :] + moe, g_ref[...], b_ref[...])


def _combine_ln(ys, dest1, dest2, x2d, meta, g, b):
    t = x2d.shape[0]
    tm = COMBINE_SUBS * TM_GATHER
    vec = pl.BlockSpec((1, D_MODEL), lambda i, d1, d2: (0, 0))
    return pl.pallas_call(
        _combine_kernel,
        grid_spec=pltpu.PrefetchScalarGridSpec(
            num_scalar_prefetch=2,
            grid=(t // tm,),
            in_specs=[pl.BlockSpec(memory_space=pl.ANY),
                      pl.BlockSpec((tm, D_MODEL), lambda i, d1, d2: (i, 0)),
                      pl.BlockSpec((tm, SUBLANES), lambda i, d1, d2: (i, 0)),
                      vec, vec],
            out_specs=pl.BlockSpec((tm, D_MODEL), lambda i, d1, d2: (i, 0)),
            scratch_shapes=[pltpu.VMEM((COMBINE_SLOTS, 2, TM_GATHER, D_MODEL), F32),
                            pltpu.SemaphoreType.DMA((COMBINE_SLOTS,))]),
        out_shape=jax.ShapeDtypeStruct((t, D_MODEL), F32),
        compiler_params=_cparams(("arbitrary",)),
        name="moe_combine_ln",
    )(dest1, dest2, ys, x2d, meta, g.reshape(1, D_MODEL), b.reshape(1, D_MODEL))


def _moe_ln(x2d, router_w, w_gate, w_up, w_down, idx, g, b):
    t = x2d.shape[0]
    meta, meta_t, counts = _router(x2d, router_w)
    counts = counts[0, :N_EXPERTS].astype(jnp.int32)
    *visits, group_start = _expert_visits(counts, (2 * t) // TM_EXPERT)

    def slot(expert_row, rank_row):
        expert = meta_t[expert_row].astype(jnp.int32)
        start = sum(jnp.where(expert == e, group_start[e], 0) for e in range(N_EXPERTS))
        return start + meta_t[rank_row].astype(jnp.int32)

    dest1 = slot(R_E1, R_RANK1)
    dest2 = slot(R_E2, R_RANK2)
    xs = _dispatch(x2d, dest1, dest2, 2 * t)
    ys = _expert_ffn(xs, tuple(visits), w_gate, w_up, w_down, idx)
    return _combine_ln(ys, dest1, dest2, x2d, meta, g, b)


def _pack_w_in(w):
    dn = 4 * DN_WIDTH
    ab = w[..., dn:dn + 2 * DN_HEADS]
    pad = jnp.zeros(w.shape[:-1] + (LANES - 2 * DN_HEADS,), w.dtype)
    packed = jnp.concatenate([w[..., :dn], w[..., dn + 2 * DN_HEADS:], ab, pad], axis=-1).astype(BF16)
    return packed, jnp.swapaxes(ab, -1, -2).astype(BF16)


@jax.jit
def _trunk(x, positions, w_in, conv_w, a_log, dt_bias, dn_norm_w, sinks, w_out, ln_g, ln_b,
           ffn_w_gate, ffn_w_up, ffn_w_down, router_w, moe_w_gate, moe_w_up, moe_w_down):
    batch, seq, _ = x.shape
    x2d = x.reshape(batch * seq, D_MODEL)
    cos, sin = _rope_tables(positions)
    w_packed, wabt = _pack_w_in(w_in)
    w_out_bf16 = w_out.astype(BF16)
    for layer in range(DEPTH):
        qkv, z, qsw, ktsw, vsw, ab, abt = _inproj(x2d, w_packed, wabt, layer, cos, sin)
        o_dn = _deltanet(qkv, z, ab, abt, conv_w[layer], a_log[layer], dt_bias[layer],
                         dn_norm_w[layer], batch, seq)
        o_sw = _swa(qsw, ktsw, vsw, sinks[layer], batch, seq)
        x2d = _outproj_ln(o_dn, o_sw, w_out_bf16, layer, x2d, ln_g[layer, 0], ln_b[layer, 0])
        i = layer // 2
        if layer % 2 == 0:
            x2d = _ffn_ln(x2d, ffn_w_gate, ffn_w_up, ffn_w_down, i, ln_g[layer, 1], ln_b[layer, 1])
        else:
            x2d = _moe_ln(x2d, router_w[i], moe_w_gate, moe_w_up, moe_w_down, i,
                          ln_g[layer, 1], ln_b[layer, 1])
    return x2d.reshape(batch, seq, D_MODEL)


def kernel(x, positions, w_in, conv_w, a_log, dt_bias, dn_norm_w, sinks, w_out, ln_g, ln_b,
           ffn_w_gate, ffn_w_up, ffn_w_down, router_w, moe_w_gate, moe_w_up, moe_w_down):
    return _trunk(x, positions, w_in, conv_w, a_log, dt_bias, dn_norm_w, sinks, w_out, ln_g, ln_b,
                  ffn_w_gate, ffn_w_up, ffn_w_down, router_w, moe_w_gate, moe_w_up, moe_w_down)
```
